```python
import math
import jax
import jax.numpy as jnp
from jax import lax
import numpy as np


D_MODEL = 2048
BATCH = 2
SEQ = 4096
DEPTH = 4

GRID_W = 64
CTX_LEN = 256
EPS = 1e-6
M_INIT = -1e30
F32 = jnp.float32

MLA_HEADS = 8
Q_LORA = 512
KV_LORA = 256
QK_NOPE = 128
QK_ROPE = 64
V_HEAD = 128
ROPE_BASE = 10000.0
Q_BLOCK = 128

ML_HEADS = 4
ML_DK = 64
ML_DV = 128
ML_CHUNK = 64

GD_HEADS = 4
GD_DK = 128
GD_DV = 128
GD_CHUNK = 64
CONV_W = 5

D_FF = 4 * D_MODEL
MLA_W = MLA_HEADS * V_HEAD
ML_W = ML_HEADS * ML_DV
GD_W = GD_HEADS * GD_DV
MIX_W = MLA_W + ML_W + GD_W

IN_SPLITS = (Q_LORA, KV_LORA, QK_ROPE,
             ML_HEADS * ML_DK, ML_HEADS * ML_DK, ML_W, ML_W, 4 * ML_HEADS,
             GD_HEADS * GD_DK, GD_HEADS * GD_DK, GD_W, GD_W, 4 * GD_HEADS)
D_IN = sum(IN_SPLITS)

kernel_name = 'hybrid_mla_mlstm_gdn_prefix_dit'


def rmsnorm(x, w):
    xf = x.astype(F32)
    y = xf * lax.rsqrt(jnp.mean(xf * xf, axis=-1, keepdims=True) + EPS)
    return (y * w.astype(F32)).astype(x.dtype)


def l2norm(x):
    xf = x.astype(F32)
    return xf * lax.rsqrt(jnp.sum(xf * xf, axis=-1, keepdims=True) + EPS)


def modulate(h, shift, scale):
    return h * (1.0 + scale) + shift


def heads(a, h):
    b, n, _ = a.shape
    return a.reshape(b, n, h, -1).transpose(0, 2, 1, 3)


def merge_heads(a):
    b, h, n, d = a.shape
    return a.transpose(0, 2, 1, 3).reshape(b, n, h * d)


def head_rmsnorm(y, gain, h):
    b, n, _ = y.shape
    return rmsnorm(y.reshape(b, n, h, -1), gain).reshape(b, n, -1)


def split_cols(p):
    return jnp.split(p, [int(i) for i in np.cumsum(IN_SPLITS)[:-1]], axis=-1)


def to_chunks(a, size):
    b, h, t = a.shape[:3]
    return jnp.moveaxis(a.reshape(b, h, t // size, size, *a.shape[3:]), 2, 0)


def from_chunks(a):
    a = jnp.moveaxis(a, 0, 2)
    return a.reshape(a.shape[0], a.shape[1], -1, *a.shape[4:])


def rope_angles(n):
    rows = n // GRID_W
    row = jnp.repeat(jnp.arange(rows, dtype=F32), GRID_W)
    col = jnp.tile(jnp.arange(GRID_W, dtype=F32), rows)
    half = QK_ROPE // 2
    inv = ROPE_BASE ** (-jnp.arange(0, half, 2, dtype=F32) / half)
    return row[:, None] * inv, col[:, None] * inv


def rotate(x, ang):
    m = ang.shape[-1]
    cos = jnp.cos(ang).astype(x.dtype)
    sin = jnp.sin(ang).astype(x.dtype)
    x1, x2 = x[..., :m], x[..., m:]
    return jnp.concatenate([x1 * cos - x2 * sin, x1 * sin + x2 * cos], axis=-1)


def axial_rope(x, ang_r, ang_c):
    half = QK_ROPE // 2
    return jnp.concatenate([rotate(x[..., :half], ang_r), rotate(x[..., half:], ang_c)], axis=-1)


def mla_qkv(c_q, c_kv, k_pe, q_norm, w_uq, kv_norm, w_ukv, rope):
    q = heads(rmsnorm(c_q, q_norm) @ w_uq, MLA_HEADS)
    kv = heads(rmsnorm(c_kv, kv_norm) @ w_ukv, MLA_HEADS)
    q_nope, q_pe = q[..., :QK_NOPE], q[..., QK_NOPE:]
    k_nope, v = kv[..., :QK_NOPE], kv[..., QK_NOPE:]
    k_pe = k_pe[:, None]
    if rope is not None:
        q_pe = axial_rope(q_pe, *rope)
        k_pe = axial_rope(k_pe, *rope)
    q = jnp.concatenate([q_nope, q_pe], axis=-1)
    k = jnp.concatenate([k_nope, jnp.broadcast_to(k_pe, k_nope.shape[:-1] + (QK_ROPE,))], axis=-1)
    return q, k, v


def softmax_attention(q, k, v):
    s = jnp.einsum('bhqd,bhkd->bhqk', q, k).astype(F32) * (QK_NOPE + QK_ROPE) ** -0.5
    p = jax.nn.softmax(s, axis=-1).astype(v.dtype)
    return jnp.einsum('bhqk,bhkd->bhqd', p, v)


def blocked_attention(q, k, v):
    b, h, n, dq = q.shape
    qb = jnp.moveaxis(q.reshape(b, h, n // Q_BLOCK, Q_BLOCK, dq), 2, 0)
    out = lax.map(lambda qi: softmax_attention(qi, k, v), qb)
    return jnp.moveaxis(out, 0, 2).reshape(b, h, n, -1)


def mlstm_inputs(p, gate_bias):
    q, k, v, o, gt = p
    b, n, _ = q.shape
    g = jnp.moveaxis((gt.astype(F32) + gate_bias.astype(F32)).reshape(b, n, 4, ML_HEADS), 1, -1)
    fwd = (g[:, 0], jax.nn.log_sigmoid(g[:, 1]))
    bwd = (g[:, 2], jax.nn.log_sigmoid(g[:, 3]))
    return (heads(q, ML_HEADS), heads(k, ML_HEADS) * ML_DK ** -0.5, heads(v, ML_HEADS), o, fwd, bwd)


def mlstm_scan(q, k, v, ig, lf, state):
    out_dtype = v.dtype
    causal = jnp.tril(jnp.ones((ML_CHUNK, ML_CHUNK), dtype=bool))
    xs = tuple(to_chunks(a.astype(F32), ML_CHUNK) for a in (q, k, v, ig, lf))

    def step(carry, inp):
        cmat, nvec, m = carry
        qc, kc, vc, ic, fc = inp
        b = jnp.cumsum(fc, axis=-1)
        log_d = jnp.where(causal, b[..., :, None] - b[..., None, :] + ic[..., None, :], -jnp.inf)
        log_inter = b + m[..., None]
        m_t = jnp.maximum(log_inter, jnp.max(log_d, axis=-1))
        d = jnp.exp(log_d - m_t[..., None])
        w_inter = jnp.exp(log_inter - m_t)
        s = jnp.einsum('bhtd,bhsd->bhts', qc, kc) * d
        num = w_inter[..., None] * jnp.einsum('bhtd,bhde->bhte', qc, cmat) + jnp.einsum('bhts,bhse->bhte', s, vc)
        den = w_inter * jnp.einsum('bhtd,bhd->bht', qc, nvec) + jnp.sum(s, axis=-1)
        h = num / jnp.maximum(jnp.abs(den), jnp.exp(-m_t))[..., None]
        b_end = b[..., -1]
        log_w = b_end[..., None] - b + ic
        m_new = jnp.maximum(b_end + m, jnp.max(log_w, axis=-1))
        decay = jnp.exp(b_end + m - m_new)
        w = jnp.exp(log_w - m_new[..., None])
        cmat = decay[..., None, None] * cmat + jnp.einsum('bhs,bhsd,bhse->bhde', w, kc, vc)
        nvec = decay[..., None] * nvec + jnp.einsum('bhs,bhsd->bhd', w, kc)
        return (cmat, nvec, m_new), h

    state, h = lax.scan(step, state, xs)
    return from_chunks(h).astype(out_dtype), state


def depthwise_conv(x, w):
    return lax.conv_general_dilated(x, w[:, None, :].astype(x.dtype), window_strides=(1,),
                                    padding=[(CONV_W // 2, CONV_W // 2)],
                                    dimension_numbers=('NWC', 'WIO', 'NWC'),
                                    feature_group_count=x.shape[-1])


def gdn_inputs(p, conv_w, a_log, dt_bias):
    q, k, v, z, ba = p
    b, n, _ = q.shape
    qkv = jax.nn.silu(depthwise_conv(jnp.concatenate([q, k, v], axis=-1), conv_w))
    nk = GD_HEADS * GD_DK
    q = l2norm(heads(qkv[..., :nk], GD_HEADS)) * GD_DK ** -0.5
    k = l2norm(heads(qkv[..., nk:2 * nk], GD_HEADS))
    v = heads(qkv[..., 2 * nk:], GD_HEADS)
    g = jnp.moveaxis(ba.astype(F32).reshape(b, n, 4, GD_HEADS), 1, -1)
    a = -jnp.exp(a_log.astype(F32))[..., None]
    dtb = dt_bias.astype(F32)[..., None]
    fwd = (jax.nn.sigmoid(g[:, 0]), a[0] * jax.nn.softplus(g[:, 1] + dtb[0]))
    bwd = (jax.nn.sigmoid(g[:, 2]), a[1] * jax.nn.softplus(g[:, 3] + dtb[1]))
    return q, k, v, z, fwd, bwd


def gdn_scan(q, k, v, beta, g, state):
    out_dtype = v.dtype
    size = GD_CHUNK
    q, k, v = (to_chunks(a.astype(F32), size) for a in (q, k, v))
    beta, g = to_chunks(beta, size), to_chunks(g, size)
    gc = jnp.cumsum(g, axis=-1)
    tril = jnp.tril(jnp.ones((size, size), dtype=bool))
    strict = jnp.tril(jnp.ones((size, size), dtype=bool), -1)
    decay = jnp.exp(jnp.where(tril, gc[..., :, None] - gc[..., None, :], -jnp.inf))
    kb = k * beta[..., None]
    t_mat = jnp.where(strict, jnp.einsum('...id,...jd->...ij', kb, k) * decay, 0.0) + jnp.eye(size, dtype=F32)
    u = lax.linalg.triangular_solve(t_mat, v * beta[..., None], left_side=True, lower=True, unit_diagonal=True)
    w = lax.linalg.triangular_solve(t_mat, kb * jnp.exp(gc)[..., None], left_side=True, lower=True, unit_diagonal=True)
    attn = jnp.einsum('...id,...jd->...ij', q, k) * decay

    def step(s_mat, inp):
        qc, kc, uc, wc, ac, gcc = inp
        v_new = uc - jnp.einsum('bhld,bhde->bhle', wc, s_mat)
        o = jnp.einsum('bhld,bhde->bhle', qc * jnp.exp(gcc)[..., None], s_mat) + jnp.einsum('bhij,bhje->bhie', ac, v_new)
        g_end = gcc[..., -1]
        s_mat = s_mat * jnp.exp(g_end)[..., None, None] + jnp.einsum(
            'bhld,bhle->bhde', kc * jnp.exp(g_end[..., None] - gcc)[..., None], v_new)
        return s_mat, o

    state, o = lax.scan(step, state, (q, k, u, w, attn, gc))
    return from_chunks(o).astype(out_dtype), state


def bidirectional(scan_fn, ctx_f, lat_f, ctx_b, lat_b, init):
    flip = lambda t: tuple(jnp.flip(a, 2) for a in t)
    hc_f, s_f = scan_fn(*ctx_f, init)
    hx_f, _ = scan_fn(*lat_f, s_f)
    hc_b, s_b = scan_fn(*flip(ctx_b), init)
    hx_b, _ = scan_fn(*flip(lat_b), s_b)
    return hc_f + jnp.flip(hc_b, 2), hx_f + jnp.flip(hx_b, 2)


def mixer_layer(hc, hx, rope, need_ctx, w_in, q_norm, w_uq, kv_norm, w_ukv, mla_norm,
                ml_bias, ml_norm, conv_w, a_log, dt_bias, gd_norm, w_out):
    pc = split_cols(hc @ w_in)
    px = split_cols(hx @ w_in)
    b = hx.shape[0]
    q_c, k_c, v_c = mla_qkv(pc[0], pc[1], pc[2], q_norm, w_uq, kv_norm, w_ukv, None)
    q_x, k_x, v_x = mla_qkv(px[0], px[1], px[2], q_norm, w_uq, kv_norm, w_ukv, rope)
    a_x = blocked_attention(q_x, jnp.concatenate([k_c, k_x], axis=2), jnp.concatenate([v_c, v_x], axis=2))
    mq_c, mk_c, mv_c, o_c, mf_c, mb_c = mlstm_inputs(pc[3:8], ml_bias)
    mq_x, mk_x, mv_x, o_x, mf_x, mb_x = mlstm_inputs(px[3:8], ml_bias)
    ml_init = (jnp.zeros((b, ML_HEADS, ML_DK, ML_DV), F32), jnp.zeros((b, ML_HEADS, ML_DK), F32),
               jnp.full((b, ML_HEADS), M_INIT, F32))
    b_c, b_x = bidirectional(mlstm_scan, (mq_c, mk_c, mv_c) + mf_c, (mq_x, mk_x, mv_x) + mf_x,
                             (mq_c, mk_c, mv_c) + mb_c, (mq_x, mk_x, mv_x) + mb_x, ml_init)
    gq_c, gk_c, gv_c, z_c, gf_c, gb_c = gdn_inputs(pc[8:13], conv_w, a_log, dt_bias)
    gq_x, gk_x, gv_x, z_x, gf_x, gb_x = gdn_inputs(px[8:13], conv_w, a_log, dt_bias)
    gd_init = jnp.zeros((b, GD_HEADS, GD_DK, GD_DV), F32)
    g_c, g_x = bidirectional(gdn_scan, (gq_c, gk_c, gv_c) + gf_c, (gq_x, gk_x, gv_x) + gf_x,
                             (gq_c, gk_c, gv_c) + gb_c, (gq_x, gk_x, gv_x) + gb_x, gd_init)

    def merge(a, m, g, o, z):
        ya = head_rmsnorm(merge_heads(a), mla_norm.reshape(MLA_HEADS, V_HEAD), MLA_HEADS)
        ym = head_rmsnorm(merge_heads(m), ml_norm.reshape(ML_HEADS, ML_DV), ML_HEADS) * jax.nn.sigmoid(o)
        yg = head_rmsnorm(merge_heads(g), gd_norm, GD_HEADS) * jax.nn.silu(z)
        return jnp.concatenate([ya, ym, yg], axis=-1) @ w_out

    y_x = merge(a_x, b_x, g_x, o_x, z_x)
    y_c = merge(softmax_attention(q_c, k_c, v_c), b_c, g_c, o_c, z_c) if need_ctx else None
    return y_c, y_x


def squared_relu_mlp(h, w1, w2):
    return jnp.square(jax.nn.relu(h @ w1)) @ w2


def setup_inputs(seed: int = 0) -> dict:
    key = jax.random.key(seed)
    ks = jax.random.split(key, 26)
    nrm = lambda k, shape, s: jax.random.normal(k, shape, F32) * s
    gain = lambda k, shape: 1.0 + 0.02 * jax.random.normal(k, shape, F32)
    L, D = DEPTH, D_MODEL
    ig_bias = nrm(ks[14], (L, 2, ML_HEADS), 0.1)
    fg_bias = 3.0 + nrm(ks[15], (L, 2, ML_HEADS), 0.5)
    ml_gate_bias = jnp.stack([ig_bias[:, 0], fg_bias[:, 0], ig_bias[:, 1], fg_bias[:, 1]], axis=1).reshape(L, 4 * ML_HEADS)
    dt = jnp.exp(jax.random.uniform(ks[19], (L, 2, GD_HEADS), F32, math.log(1e-3), math.log(1e-1)))
    return {
        'x': nrm(ks[0], (BATCH, SEQ, D), 1.0),
        'c': nrm(ks[1], (BATCH, D), 1.0),
        'ctx': nrm(ks[2], (BATCH, CTX_LEN, D), 1.0),
        'c_ctx': nrm(ks[3], (D,), 1.0),
        'w_ada': nrm(ks[4], (L, D, 6 * D), 0.5 * D ** -0.5),
        'b_ada': nrm(ks[5], (L, 6 * D), 0.02),
        'norm1': gain(ks[6], (L, D)),
        'norm2': gain(ks[7], (L, D)),
        'w_in': nrm(ks[8], (L, D, D_IN), D ** -0.5),
        'mla_q_norm': gain(ks[9], (L, Q_LORA)),
        'mla_w_uq': nrm(ks[10], (L, Q_LORA, MLA_HEADS * (QK_NOPE + QK_ROPE)), Q_LORA ** -0.5),
        'mla_kv_norm': gain(ks[11], (L, KV_LORA)),
        'mla_w_ukv': nrm(ks[12], (L, KV_LORA, MLA_HEADS * (QK_NOPE + V_HEAD)), KV_LORA ** -0.5),
        'mla_out_norm': gain(ks[13], (L, MLA_W)),
        'ml_gate_bias': ml_gate_bias,
        'ml_out_norm': gain(ks[16], (L, ML_W)),
        'gd_conv': nrm(ks[17], (L, CONV_W, 2 * GD_HEADS * GD_DK + GD_W), CONV_W ** -0.5),
        'gd_a_log': jnp.log(jax.random.uniform(ks[18], (L, 2, GD_HEADS), F32, 1.0, 16.0)),
        'gd_dt_bias': dt + jnp.log(-jnp.expm1(-dt)),
        'gd_out_norm': gain(ks[20], (L, GD_DV)),
        'w_out': nrm(ks[21], (L, MIX_W, D), MIX_W ** -0.5),
        'w_mlp1': nrm(ks[22], (L, D, D_FF), D ** -0.5),
        'w_mlp2': nrm(ks[23], (L, D_FF, D), D_FF ** -0.5),
        'final_norm': gain(ks[24], (D,)),
    }


def reference(x, c, ctx, c_ctx, w_ada, b_ada, norm1, norm2, w_in, mla_q_norm, mla_w_uq,
              mla_kv_norm, mla_w_ukv, mla_out_norm, ml_gate_bias, ml_out_norm, gd_conv,
              gd_a_log, gd_dt_bias, gd_out_norm, w_out, w_mlp1, w_mlp2, final_norm):
    rope = rope_angles(x.shape[1])
    s_lat = jax.nn.silu(c)
    s_ctx = jax.nn.silu(c_ctx)[None]
    xc = ctx
    for l in range(DEPTH):
        last = l == DEPTH - 1
        mx = jnp.split((s_lat @ w_ada[l] + b_ada[l])[:, None, :], 6, axis=-1)
        mc = jnp.split((s_ctx @ w_ada[l] + b_ada[l])[:, None, :], 6, axis=-1)
        hx = modulate(rmsnorm(x, norm1[l]), mx[0], mx[1])
        hc = modulate(rmsnorm(xc, norm1[l]), mc[0], mc[1])
        y_c, y_x = mixer_layer(hc, hx, rope, not last, w_in[l], mla_q_norm[l], mla_w_uq[l],
                               mla_kv_norm[l], mla_w_ukv[l], mla_out_norm[l], ml_gate_bias[l],
                               ml_out_norm[l], gd_conv[l], gd_a_log[l], gd_dt_bias[l],
                               gd_out_norm[l], w_out[l])
        x = x + mx[2] * y_x
        x = x + mx[5] * squared_relu_mlp(modulate(rmsnorm(x, norm2[l]), mx[3], mx[4]), w_mlp1[l], w_mlp2[l])
        if not last:
            xc = xc + mc[2] * y_c
            xc = xc + mc[5] * squared_relu_mlp(modulate(rmsnorm(xc, norm2[l]), mc[3], mc[4]), w_mlp1[l], w_mlp2[l])
    return rmsnorm(x, final_norm)
```

```python
import functools

import jax
import jax.numpy as jnp
from jax import lax
from jax.experimental import pallas as pl
from jax.experimental.pallas import tpu as pltpu

F32 = jnp.float32
BF16 = jnp.bfloat16

EPS = 1e-6
M_INIT = -1e30
GRID_W = 64
ROPE_BASE = 10000.0

MLA_HEADS = 8
Q_LORA = 512
KV_LORA = 256
QK_NOPE = 128
QK_ROPE = 64
V_HEAD = 128
QK_DIM = QK_NOPE + QK_ROPE
ML_HEADS = 4
ML_DK = 64
ML_DV = 128
GD_HEADS = 4
GD_DK = 128
GD_DV = 128
CHUNK = 64
CONV_W = 5

MLA_W = MLA_HEADS * V_HEAD
ML_W = ML_HEADS * ML_DV
GD_W = GD_HEADS * GD_DV
MIX_W = MLA_W + ML_W + GD_W

P_MLA = 896
P_ML = 1536
P_GD = 2048
P_GATE = 128
P_ALL = P_MLA + P_ML + P_GD + P_GATE

TM = 512
TB = 128
HALO = 16
LANES = 128

VMEM_LIMIT_V7X = 56 * 1024 * 1024

NT_DIMS = (((1,), (1,)), ((), ()))
TN_DIMS = (((0,), (0,)), ((), ()))
NN_DIMS = (((1,), (0,)), ((), ()))


def _cparams(*sem):
    return pltpu.CompilerParams(dimension_semantics=sem, vmem_limit_bytes=VMEM_LIMIT_V7X)


def _sigmoid(x):
    return 1.0 / (1.0 + jnp.exp(-x))


def _rms(x):
    return x * lax.rsqrt(jnp.mean(x * x, axis=-1, keepdims=True) + EPS)


def _norm_mod(x, norm_w, shift, scale):
    return _rms(x) * norm_w * (1.0 + scale) + shift


def _mod_row(i, seq, batch):
    return jnp.minimum((i * TM) // seq, batch)


def _ada_kernel(cb_ref, w_ref, b_ref, o_ref):
    rows = cb_ref.shape[0]
    tn = w_ref.shape[-1]
    for r in range(rows):
        cb = cb_ref[r]
        s = cb * _sigmoid(cb)
        for j in range(tn // LANES):
            cols = slice(j * LANES, (j + 1) * LANES)
            acc = jnp.sum(w_ref[:, cols] * s, axis=0, keepdims=True)
            o_ref[r:r + 1, cols] = acc + b_ref[:, cols]


def _ada(cvec, w_ada, b_ada):
    nl, d, n6 = w_ada.shape
    r = cvec.shape[0]
    tn = 512
    cb = jnp.broadcast_to(cvec[:, :, None], (r, d, LANES))
    return pl.pallas_call(
        _ada_kernel,
        grid=(nl, n6 // tn),
        in_specs=[
            pl.BlockSpec((r, d, LANES), lambda l, j: (0, 0, 0)),
            pl.BlockSpec((None, d, tn), lambda l, j: (l, 0, j)),
            pl.BlockSpec((None, 1, tn), lambda l, j: (l, 0, j)),
        ],
        out_specs=pl.BlockSpec((None, r, tn), lambda l, j: (l, 0, j)),
        out_shape=jax.ShapeDtypeStruct((nl, r, n6), F32),
        compiler_params=_cparams("parallel", "parallel"),
        name="ada",
    )(cb, w_ada, b_ada.reshape(nl, 1, n6))


def _inproj_kernel(x_ref, mod_ref, nw_ref, w_ref, omla_ref, oml_ref, ogd_ref, og_ref):
    h = _norm_mod(x_ref[...], nw_ref[...], mod_ref[0:1, :], mod_ref[1:2, :]).astype(BF16)
    c0 = 0
    for o_ref in (omla_ref, oml_ref, ogd_ref, og_ref):
        n = o_ref.shape[-1]
        o_ref[...] = jnp.dot(h, w_ref[:, c0:c0 + n], preferred_element_type=F32).astype(o_ref.dtype)
        c0 += n


def _inproj(xg, mod, norm_w, w, seq, batch):
    nt, d = xg.shape
    row = functools.partial(_mod_row, seq=seq, batch=batch)
    widths = (P_MLA, P_ML, P_GD, P_GATE)
    dtypes = (BF16, BF16, BF16, F32)
    return pl.pallas_call(
        _inproj_kernel,
        grid=(nt // TM,),
        in_specs=[
            pl.BlockSpec((TM, d), lambda i: (i, 0)),
            pl.BlockSpec((None, 6, d), lambda i: (row(i), 0, 0)),
            pl.BlockSpec((1, d), lambda i: (0, 0)),
            pl.BlockSpec((d, P_ALL), lambda i: (0, 0), pipeline_mode=pl.Buffered(1)),
        ],
        out_specs=[pl.BlockSpec((TM, n), lambda i: (i, 0)) for n in widths],
        out_shape=[jax.ShapeDtypeStruct((nt, n), dt) for n, dt in zip(widths, dtypes)],
        compiler_params=_cparams("parallel"),
        name="inproj",
    )(xg, mod, norm_w.reshape(1, d), w)


def _swap16(x):
    n = x.shape[-1]
    lane = lax.broadcasted_iota(jnp.int32, x.shape, 1)
    return jnp.where(lane % 32 < 16, pltpu.roll(x, n - 16, 1), pltpu.roll(x, 16, 1))


def _mla_qkv_kernel(p_ref, qn_ref, kvn_ref, wuq_ref, wukv_ref, cos_ref, sin_ref, q_ref, k_ref, v_ref):
    p = p_ref[...].astype(F32)
    cq = p[:, :Q_LORA]
    ckv = p[:, Q_LORA:Q_LORA + KV_LORA]
    kpe = p[:, Q_LORA + KV_LORA:]
    cos = cos_ref[...]
    sin = sin_ref[...]
    qn = (_rms(cq) * qn_ref[...]).astype(BF16)
    q = jnp.dot(qn, wuq_ref[...], preferred_element_type=F32) * (QK_DIM ** -0.5)
    nope_w = MLA_HEADS * QK_NOPE
    qpe = q[:, nope_w:]
    qpe = qpe * cos + _swap16(qpe) * sin
    kvn = (_rms(ckv) * kvn_ref[...]).astype(BF16)
    kv = jnp.dot(kvn, wukv_ref[...], preferred_element_type=F32)
    kpe = kpe * cos[:, :LANES] + _swap16(kpe) * sin[:, :LANES]
    kpe = kpe[:, :QK_ROPE]
    for h in range(MLA_HEADS):
        qh = jnp.concatenate([q[:, h * QK_NOPE:(h + 1) * QK_NOPE], qpe[:, h * QK_ROPE:(h + 1) * QK_ROPE]], axis=-1)
        kh = jnp.concatenate([kv[:, h * QK_NOPE:(h + 1) * QK_NOPE], kpe], axis=-1)
        q_ref[h] = qh.astype(BF16)
        k_ref[h] = kh.astype(BF16)
        v_ref[h] = kv[:, nope_w + h * V_HEAD:nope_w + (h + 1) * V_HEAD].astype(BF16)


def _mla_qkv_call(p_mla, q_norm, kv_norm, wuq, wukv, cos_t, sin_t, seq, n_lat_rows):
    nt = p_mla.shape[0]
    lat_tiles = seq // TM
    nl_tiles = n_lat_rows // TM
    pe_w = MLA_HEADS * QK_ROPE

    def rope_block(i):
        return (jnp.where(i < nl_tiles, i % lat_tiles, lat_tiles), 0)

    return pl.pallas_call(
        _mla_qkv_kernel,
        grid=(nt // TM,),
        in_specs=[
            pl.BlockSpec((TM, P_MLA), lambda i: (i, 0)),
            pl.BlockSpec((1, Q_LORA), lambda i: (0, 0)),
            pl.BlockSpec((1, KV_LORA), lambda i: (0, 0)),
            pl.BlockSpec(wuq.shape, lambda i: (0, 0)),
            pl.BlockSpec(wukv.shape, lambda i: (0, 0)),
            pl.BlockSpec((TM, pe_w), rope_block),
            pl.BlockSpec((TM, pe_w), rope_block),
        ],
        out_specs=[
            pl.BlockSpec((MLA_HEADS, TM, QK_DIM), lambda i: (0, i, 0)),
            pl.BlockSpec((MLA_HEADS, TM, QK_DIM), lambda i: (0, i, 0)),
            pl.BlockSpec((MLA_HEADS, TM, V_HEAD), lambda i: (0, i, 0)),
        ],
        out_shape=[
            jax.ShapeDtypeStruct((MLA_HEADS, nt, QK_DIM), BF16),
            jax.ShapeDtypeStruct((MLA_HEADS, nt, QK_DIM), BF16),
            jax.ShapeDtypeStruct((MLA_HEADS, nt, V_HEAD), BF16),
        ],
        compiler_params=_cparams("parallel"),
        name="mla_qkv",
    )(p_mla, q_norm.reshape(1, -1), kv_norm.reshape(1, -1), wuq, wukv, cos_t, sin_t)


def _attn_lat_kernel(q_ref, kl_ref, vl_ref, kc_ref, vc_ref, o_ref):
    q = q_ref[...]
    s1 = lax.dot_general(q, kl_ref[...], NT_DIMS, preferred_element_type=F32)
    s2 = lax.dot_general(q, kc_ref[...], NT_DIMS, preferred_element_type=F32)
    m = jnp.maximum(jnp.max(s1, axis=-1, keepdims=True), jnp.max(s2, axis=-1, keepdims=True))
    p1 = jnp.exp(s1 - m)
    p2 = jnp.exp(s2 - m)
    l = jnp.sum(p1, axis=-1, keepdims=True) + jnp.sum(p2, axis=-1, keepdims=True)
    o = jnp.dot(p1.astype(BF16), vl_ref[...], preferred_element_type=F32)
    o = o + jnp.dot(p2.astype(BF16), vc_ref[...], preferred_element_type=F32)
    o_ref[...] = (o / l).astype(o_ref.dtype)


def _attn_ctx_kernel(q_ref, kc_ref, vc_ref, o_ref):
    s = lax.dot_general(q_ref[...], kc_ref[...], NT_DIMS, preferred_element_type=F32)
    m = jnp.max(s, axis=-1, keepdims=True)
    p = jnp.exp(s - m)
    l = jnp.sum(p, axis=-1, keepdims=True)
    o = jnp.dot(p.astype(BF16), vc_ref[...], preferred_element_type=F32)
    o_ref[...] = (o / l).astype(o_ref.dtype)


def _attention(q, k, v, batch, seq, nctx):
    tq = TM
    nq = seq // tq
    ctx0 = batch * seq // nctx
    a_lat = pl.pallas_call(
        _attn_lat_kernel,
        grid=(batch, MLA_HEADS, nq),
        in_specs=[
            pl.BlockSpec((None, tq, QK_DIM), lambda b, h, i: (h, b * nq + i, 0)),
            pl.BlockSpec((None, seq, QK_DIM), lambda b, h, i: (h, b, 0)),
            pl.BlockSpec((None, seq, V_HEAD), lambda b, h, i: (h, b, 0)),
            pl.BlockSpec((None, nctx, QK_DIM), lambda b, h, i: (h, ctx0 + b, 0)),
            pl.BlockSpec((None, nctx, V_HEAD), lambda b, h, i: (h, ctx0 + b, 0)),
        ],
        out_specs=pl.BlockSpec((tq, V_HEAD), lambda b, h, i: (b * nq + i, h)),
        out_shape=jax.ShapeDtypeStruct((batch * seq, MLA_W), F32),
        compiler_params=_cparams("parallel", "parallel", "arbitrary"),
        name="attn_lat",
    )(q, k, v, k, v)
    a_ctx = pl.pallas_call(
        _attn_ctx_kernel,
        grid=(batch, MLA_HEADS),
        in_specs=[
            pl.BlockSpec((None, nctx, QK_DIM), lambda b, h: (h, ctx0 + b, 0)),
            pl.BlockSpec((None, nctx, QK_DIM), lambda b, h: (h, ctx0 + b, 0)),
            pl.BlockSpec((None, nctx, V_HEAD), lambda b, h: (h, ctx0 + b, 0)),
        ],
        out_specs=pl.BlockSpec((nctx, V_HEAD), lambda b, h: (b, h)),
        out_shape=jax.ShapeDtypeStruct((batch * nctx, MLA_W), F32),
        compiler_params=_cparams("parallel", "parallel"),
        name="attn_ctx",
    )(q, k, v)
    return jnp.concatenate([a_lat, a_ctx], axis=0)


def _gates_kernel(g_ref, par_ref, col_ref, row_ref):
    tb = g_ref.shape[0]
    z = g_ref[...] + par_ref[0:1, :]
    neg_a = -jnp.exp(par_ref[1:2, :])
    lane = lax.broadcasted_iota(jnp.int32, z.shape, 1)
    kind = (lane // 4) % 4
    is_ml = lane < 16
    is_cum = (kind % 2 == 1) & (lane < 32)
    is_bwd = kind >= 2
    soft = jnp.log(1.0 + jnp.exp(-jnp.abs(z)))
    log_sig = jnp.minimum(z, 0.0) - soft
    softplus = jnp.maximum(z, 0.0) + soft
    pre = jnp.where(is_ml, jnp.where(is_cum, log_sig, z), jnp.where(is_cum, neg_a * softplus, _sigmoid(z)))
    r = lax.broadcasted_iota(jnp.int32, (tb, tb), 0)
    c = lax.broadcasted_iota(jnp.int32, (tb, tb), 1)
    same = (r // CHUNK) == (c // CHUNK)
    lower = jnp.where(same & (c <= r), 1.0, 0.0).astype(F32)
    upper = jnp.where(same & (c >= r), 1.0, 0.0).astype(F32)
    cum_f = jnp.dot(lower, pre, preferred_element_type=F32, precision=lax.Precision.HIGHEST)
    cum_b = jnp.dot(upper, pre, preferred_element_type=F32, precision=lax.Precision.HIGHEST)
    out = jnp.where(is_cum, jnp.where(is_bwd, cum_b, cum_f), pre)
    col_ref[...] = out
    row_ref[...] = out.T


def _gates(g, par):
    nt = g.shape[0]
    return pl.pallas_call(
        _gates_kernel,
        grid=(nt // TB,),
        in_specs=[
            pl.BlockSpec((TB, P_GATE), lambda i: (i, 0)),
            pl.BlockSpec((8, P_GATE), lambda i: (0, 0)),
        ],
        out_specs=[
            pl.BlockSpec((TB, P_GATE), lambda i: (i, 0)),
            pl.BlockSpec((P_GATE, TB), lambda i: (0, i)),
        ],
        out_shape=[
            jax.ShapeDtypeStruct((nt, P_GATE), F32),
            jax.ShapeDtypeStruct((P_GATE, nt), F32),
        ],
        compiler_params=_cparams("parallel"),
        name="gates",
    )(g, par)


def _scan_blocks(batch, seq, nctx):
    ncb = nctx // TB
    nlb = seq // TB
    ctx0 = batch * nlb

    def fwd(b, j):
        return jnp.where(j < ncb, ctx0 + b * ncb + j, b * nlb + (j - ncb))

    def bwd(b, j):
        return jnp.where(j < ncb, ctx0 + b * ncb + (ncb - 1 - j), b * nlb + (nlb - 1 - (j - ncb)))

    return ncb + nlb, fwd, bwd


def _mlstm_kernel(xf_ref, xb_ref, gcf_ref, gcb_ref, grf_ref, grb_ref, hf_ref, hb_ref, c_ref, n_ref, m_ref):
    @pl.when(pl.program_id(1) == 0)
    def _():
        c_ref[...] = jnp.zeros_like(c_ref)
        n_ref[...] = jnp.zeros_like(n_ref)
        m_ref[...] = jnp.full(m_ref.shape, M_INIT, F32)

    nch = TB // CHUNK
    row = lax.broadcasted_iota(jnp.int32, (CHUNK, CHUNK), 0)
    col = lax.broadcasted_iota(jnp.int32, (CHUNK, CHUNK), 1)
    kscale = ML_DK ** -0.5
    qw = ML_HEADS * ML_DK
    for ci in range(nch):
        for d in range(2):
            x_ref, gc_ref, gr_ref, out_ref = ((xf_ref, gcf_ref, grf_ref, hf_ref), (xb_ref, gcb_ref, grb_ref, hb_ref))[d]
            ch = ci if d == 0 else nch - 1 - ci
            rs = slice(ch * CHUNK, (ch + 1) * CHUNK)
            mask = (col <= row) if d == 0 else (col >= row)
            for h in range(ML_HEADS):
                q = x_ref[rs, h * ML_DK:(h + 1) * ML_DK]
                k = x_ref[rs, qw + h * ML_DK:qw + (h + 1) * ML_DK]
                v = x_ref[rs, 2 * qw + h * ML_DV:2 * qw + (h + 1) * ML_DV]
                gi = d * 8 + h
                gb = d * 8 + 4 + h
                i_col = gc_ref[rs, gi:gi + 1]
                b_col = gc_ref[rs, gb:gb + 1]
                i_row = gr_ref[gi:gi + 1, rs]
                b_row = gr_ref[gb:gb + 1, rs]
                b_end = b_col[CHUNK - 1:CHUNK, :] if d == 0 else b_col[0:1, :]
                st = d * ML_HEADS + h
                m = m_ref[st][:, 0:1]
                cmat = c_ref[st]
                nrow = n_ref[st]

                log_d = jnp.where(mask, b_col - b_row + i_row, -jnp.inf)
                log_inter = b_col + m
                m_t = jnp.maximum(log_inter, jnp.max(log_d, axis=-1, keepdims=True))
                dmat = jnp.exp(log_d - m_t)
                w_inter = jnp.exp(log_inter - m_t)
                s = lax.dot_general(q, k, NT_DIMS, preferred_element_type=F32) * dmat * kscale
                num = w_inter * jnp.dot(q, cmat.astype(BF16), preferred_element_type=F32)
                num = num + jnp.dot(s.astype(BF16), v, preferred_element_type=F32)
                den = w_inter * jnp.sum(q.astype(F32) * nrow, axis=-1, keepdims=True)
                den = den + jnp.sum(s, axis=-1, keepdims=True)
                out_ref[rs, h * ML_DV:(h + 1) * ML_DV] = num / jnp.maximum(jnp.abs(den), jnp.exp(-m_t))

                log_w = b_end - b_col + i_col
                m_new = jnp.maximum(b_end + m, jnp.max(log_w, axis=0, keepdims=True))
                decay = jnp.exp(b_end + m - m_new)
                kw = k.astype(F32) * (jnp.exp(log_w - m_new) * kscale)
                c_ref[st] = decay * cmat + lax.dot_general(kw.astype(BF16), v, TN_DIMS, preferred_element_type=F32)
                n_ref[st] = decay * nrow + jnp.sum(kw, axis=0, keepdims=True)
                m_ref[st] = jnp.broadcast_to(m_new, (1, LANES))


def _mlstm(p_ml, gcol, grow, batch, seq, nctx):
    nt = p_ml.shape[0]
    nsteps, fwd, bwd = _scan_blocks(batch, seq, nctx)
    xw = 2 * ML_HEADS * ML_DK + ML_W
    nst = 2 * ML_HEADS
    return pl.pallas_call(
        _mlstm_kernel,
        grid=(batch, nsteps),
        in_specs=[
            pl.BlockSpec((TB, xw), lambda b, j: (fwd(b, j), 0)),
            pl.BlockSpec((TB, xw), lambda b, j: (bwd(b, j), 0)),
            pl.BlockSpec((TB, P_GATE), lambda b, j: (fwd(b, j), 0)),
            pl.BlockSpec((TB, P_GATE), lambda b, j: (bwd(b, j), 0)),
            pl.BlockSpec((P_GATE, TB), lambda b, j: (0, fwd(b, j))),
            pl.BlockSpec((P_GATE, TB), lambda b, j: (0, bwd(b, j))),
        ],
        out_specs=[
            pl.BlockSpec((TB, ML_W), lambda b, j: (fwd(b, j), 0)),
            pl.BlockSpec((TB, ML_W), lambda b, j: (bwd(b, j), 0)),
        ],
        out_shape=[jax.ShapeDtypeStruct((nt, ML_W), F32)] * 2,
        scratch_shapes=[
            pltpu.VMEM((nst, ML_DK, ML_DV), F32),
            pltpu.VMEM((nst, 1, ML_DK), F32),
            pltpu.VMEM((nst, 1, LANES), F32),
        ],
        compiler_params=_cparams("parallel", "arbitrary"),
        name="mlstm",
    )(p_ml, p_ml, gcol, gcol, grow, grow)


def _gdn_prep_kernel(x_ref, prev_ref, next_ref, w_ref, o_ref, buf_ref, *, seq_blocks, lat_blocks):
    i = pl.program_id(0)
    in_lat = i < lat_blocks
    first = jnp.where(in_lat, i % seq_blocks == 0, True)
    last = jnp.where(in_lat, i % seq_blocks == seq_blocks - 1, True)
    tb = x_ref.shape[0]
    prev = prev_ref[...].astype(F32)[HALO - 8:, :]
    nxt = next_ref[...].astype(F32)[:8, :]
    buf_ref[0:8, :] = jnp.where(first, 0.0, prev)
    buf_ref[8:8 + tb, :] = x_ref[...].astype(F32)
    buf_ref[8 + tb:16 + tb, :] = jnp.where(last, 0.0, nxt)
    half = CONV_W // 2
    y = None
    for j in range(CONV_W):
        term = buf_ref[8 - half + j:8 - half + j + tb, :] * w_ref[j:j + 1, :]
        y = term if y is None else y + term
    y = y * _sigmoid(y)
    nk = GD_HEADS * GD_DK
    for h in range(2 * GD_HEADS):
        cols = slice(h * GD_DK, (h + 1) * GD_DK)
        seg = y[:, cols]
        seg = seg * lax.rsqrt(jnp.sum(seg * seg, axis=-1, keepdims=True) + EPS)
        if h < GD_HEADS:
            seg = seg * (GD_DK ** -0.5)
        o_ref[:, cols] = seg
    o_ref[:, 2 * nk:] = y[:, 2 * nk:]


def _gdn_prep(p_gd, conv_w, batch, seq, nctx):
    nt = p_gd.shape[0]
    cw = 2 * GD_HEADS * GD_DK + GD_W
    tb = nctx
    hb = tb // HALO
    nblk = nt // tb
    kern = functools.partial(_gdn_prep_kernel, seq_blocks=seq // tb, lat_blocks=batch * seq // tb)
    return pl.pallas_call(
        kern,
        grid=(nblk,),
        in_specs=[
            pl.BlockSpec((tb, cw), lambda i: (i, 0)),
            pl.BlockSpec((HALO, cw), lambda i: (jnp.maximum(i * hb - 1, 0), 0)),
            pl.BlockSpec((HALO, cw), lambda i: (jnp.minimum((i + 1) * hb, nblk * hb - 1), 0)),
            pl.BlockSpec((CONV_W, cw), lambda i: (0, 0)),
        ],
        out_specs=pl.BlockSpec((tb, cw), lambda i: (i, 0)),
        out_shape=jax.ShapeDtypeStruct((nt, cw), F32),
        scratch_shapes=[pltpu.VMEM((tb + 16, cw), F32)],
        compiler_params=_cparams("parallel"),
        name="gdn_prep",
    )(p_gd, p_gd, p_gd, conv_w)


def _dot(a, b, dims=NN_DIMS):
    return lax.dot_general(a.astype(BF16), b.astype(BF16), dims, preferred_element_type=F32)


INV_BLOCK = 16


def _unit_lower_inverse(a):
    n = a.shape[0]
    row = lax.broadcasted_iota(jnp.int32, (n, n), 0)
    col = lax.broadcasted_iota(jnp.int32, (n, n), 1)
    eye = jnp.where(row == col, 1.0, 0.0)
    p = jnp.where(row // INV_BLOCK == col // INV_BLOCK, a, 0.0)
    x = eye - p
    for _ in range((INV_BLOCK - 1).bit_length() - 1):
        p = _dot(p, p)
        x = x + _dot(x, p)
    size = INV_BLOCK
    while size < n:
        off = jnp.where((row // (2 * size) == col // (2 * size)) & (row // size != col // size), a, 0.0)
        x = x - _dot(_dot(x, off), x)
        size *= 2
    return x


def _gdn_kernel(xf_ref, xb_ref, gcf_ref, gcb_ref, grf_ref, grb_ref, of_ref, ob_ref, s_ref):
    @pl.when(pl.program_id(1) == 0)
    def _():
        s_ref[...] = jnp.zeros_like(s_ref)

    nch = TB // CHUNK
    row = lax.broadcasted_iota(jnp.int32, (CHUNK, CHUNK), 0)
    col = lax.broadcasted_iota(jnp.int32, (CHUNK, CHUNK), 1)
    nk = GD_HEADS * GD_DK
    for ci in range(nch):
        for d in range(2):
            x_ref, gc_ref, gr_ref, out_ref = ((xf_ref, gcf_ref, grf_ref, of_ref), (xb_ref, gcb_ref, grb_ref, ob_ref))[d]
            ch = ci if d == 0 else nch - 1 - ci
            rs = slice(ch * CHUNK, (ch + 1) * CHUNK)
            incl = (col <= row) if d == 0 else (col >= row)
            strict = (col < row) if d == 0 else (col > row)
            for h in range(GD_HEADS):
                q = x_ref[rs, h * GD_DK:(h + 1) * GD_DK]
                k = x_ref[rs, nk + h * GD_DK:nk + (h + 1) * GD_DK]
                v = x_ref[rs, 2 * nk + h * GD_DV:2 * nk + (h + 1) * GD_DV]
                cb = 16 + d * 8 + h
                cg = 16 + d * 8 + 4 + h
                beta = gc_ref[rs, cb:cb + 1]
                g_col = gc_ref[rs, cg:cg + 1]
                g_row = gr_ref[cg:cg + 1, rs]
                g_end = g_col[CHUNK - 1:CHUNK, :] if d == 0 else g_col[0:1, :]
                st = d * GD_HEADS + h
                smat = s_ref[st]

                decay = jnp.exp(jnp.where(incl, g_col - g_row, -jnp.inf))
                kb = k * beta
                kk = _dot(kb, k, NT_DIMS)
                tinv = _unit_lower_inverse(jnp.where(strict, kk * decay, 0.0))
                rhs = jnp.concatenate([v * beta, kb * jnp.exp(g_col)], axis=-1)
                uw = _dot(tinv, rhs)
                u = uw[:, :GD_DV]
                w = uw[:, GD_DV:]
                attn = _dot(q, k, NT_DIMS) * decay
                v_new = u - _dot(w, smat)
                o = _dot(q * jnp.exp(g_col), smat) + _dot(attn, v_new)
                out_ref[rs, h * GD_DV:(h + 1) * GD_DV] = o
                kd = k * jnp.exp(g_end - g_col)
                s_ref[st] = smat * jnp.exp(g_end) + _dot(kd, v_new, TN_DIMS)


def _gdn(qkv, gcol, grow, batch, seq, nctx):
    nt = qkv.shape[0]
    nsteps, fwd, bwd = _scan_blocks(batch, seq, nctx)
    cw = qkv.shape[1]
    return pl.pallas_call(
        _gdn_kernel,
        grid=(batch, nsteps),
        in_specs=[
            pl.BlockSpec((TB, cw), lambda b, j: (fwd(b, j), 0)),
            pl.BlockSpec((TB, cw), lambda b, j: (bwd(b, j), 0)),
            pl.BlockSpec((TB, P_GATE), lambda b, j: (fwd(b, j), 0)),
            pl.BlockSpec((TB, P_GATE), lambda b, j: (bwd(b, j), 0)),
            pl.BlockSpec((P_GATE, TB), lambda b, j: (0, fwd(b, j))),
            pl.BlockSpec((P_GATE, TB), lambda b, j: (0, bwd(b, j))),
        ],
        out_specs=[
            pl.BlockSpec((TB, GD_W), lambda b, j: (fwd(b, j), 0)),
            pl.BlockSpec((TB, GD_W), lambda b, j: (bwd(b, j), 0)),
        ],
        out_shape=[jax.ShapeDtypeStruct((nt, GD_W), F32)] * 2,
        scratch_shapes=[pltpu.VMEM((2 * GD_HEADS, GD_DK, GD_DV), F32)],
        compiler_params=_cparams("parallel", "arbitrary"),
        name="gdn",
    )(qkv, qkv, gcol, gcol, grow, grow)


def _head_rms(y, gain, nheads, width):
    parts = []
    for h in range(nheads):
        seg = y[:, h * width:(h + 1) * width]
        parts.append(_rms(seg))
    return jnp.concatenate(parts, axis=-1) * gain


def _merge_kernel(x_ref, mod_ref, a_ref, hf_ref, hb_ref, o_ref, gf_ref, gb_ref, z_ref,
                  an_ref, mn_ref, gn_ref, w_ref, out_ref):
    ya = _head_rms(a_ref[...], an_ref[...], MLA_HEADS, V_HEAD)
    ym = _head_rms(hf_ref[...] + hb_ref[...], mn_ref[...], ML_HEADS, ML_DV) * _sigmoid(o_ref[...].astype(F32))
    z = z_ref[...].astype(F32)
    yg = _head_rms(gf_ref[...] + gb_ref[...], gn_ref[...], GD_HEADS, GD_DV) * (z * _sigmoid(z))
    y = jnp.dot(ya.astype(BF16), w_ref[0:MLA_W, :], preferred_element_type=F32)
    y = y + jnp.dot(ym.astype(BF16), w_ref[MLA_W:MLA_W + ML_W, :], preferred_element_type=F32)
    y = y + jnp.dot(yg.astype(BF16), w_ref[MLA_W + ML_W:, :], preferred_element_type=F32)
    out_ref[...] = x_ref[...] + mod_ref[2:3, :] * y


def _merge(xg, mod, a, hf, hb, p_ml, gf, gb, p_gd, mla_norm, ml_norm, gd_norm, w_out, seq, batch, nrows):
    d = xg.shape[1]
    row = functools.partial(_mod_row, seq=seq, batch=batch)
    tile = lambda w, cb=0: pl.BlockSpec((TM, w), lambda i: (i, cb))
    vec = lambda w: pl.BlockSpec((1, w), lambda i: (0, 0))
    return pl.pallas_call(
        _merge_kernel,
        grid=(nrows // TM,),
        in_specs=[
            tile(d),
            pl.BlockSpec((None, 6, d), lambda i: (row(i), 0, 0)),
            tile(MLA_W), tile(ML_W), tile(ML_W), tile(ML_W, 2),
            tile(GD_W), tile(GD_W), tile(GD_W, 3),
            vec(MLA_W), vec(ML_W), vec(GD_W),
            pl.BlockSpec((MIX_W, d), lambda i: (0, 0)),
        ],
        out_specs=tile(d),
        out_shape=jax.ShapeDtypeStruct((nrows, d), F32),
        compiler_params=_cparams("parallel"),
        name="merge",
    )(xg, mod, a, hf, hb, p_ml, gf, gb, p_gd, mla_norm.reshape(1, -1), ml_norm.reshape(1, -1),
      jnp.tile(gd_norm, GD_HEADS).reshape(1, -1), w_out)


def _mlp_kernel(x_ref, mod_ref, nw_ref, w1_ref, w2_ref, out_ref, h_ref, acc_ref):
    j = pl.program_id(1)

    @pl.when(j == 0)
    def _():
        h_ref[...] = _norm_mod(x_ref[...], nw_ref[...], mod_ref[3:4, :], mod_ref[4:5, :]).astype(BF16)
        acc_ref[...] = jnp.zeros_like(acc_ref)

    a = jnp.maximum(jnp.dot(h_ref[...], w1_ref[...], preferred_element_type=F32), 0.0)
    acc_ref[...] += jnp.dot((a * a).astype(BF16), w2_ref[...], preferred_element_type=F32)

    @pl.when(j == pl.num_programs(1) - 1)
    def _():
        out_ref[...] = x_ref[...] + mod_ref[5:6, :] * acc_ref[...]


def _mlp(xg, mod, norm_w, w1, w2, seq, batch, nrows):
    d = xg.shape[1]
    f = w1.shape[1]
    tf = 512
    row = functools.partial(_mod_row, seq=seq, batch=batch)
    return pl.pallas_call(
        _mlp_kernel,
        grid=(nrows // TM, f // tf),
        in_specs=[
            pl.BlockSpec((TM, d), lambda i, j: (i, 0)),
            pl.BlockSpec((None, 6, d), lambda i, j: (row(i), 0, 0)),
            pl.BlockSpec((1, d), lambda i, j: (0, 0)),
            pl.BlockSpec((d, tf), lambda i, j: (0, j)),
            pl.BlockSpec((tf, d), lambda i, j: (j, 0)),
        ],
        out_specs=pl.BlockSpec((TM, d), lambda i, j: (i, 0)),
        out_shape=jax.ShapeDtypeStruct((nrows, d), F32),
        scratch_shapes=[pltpu.VMEM((TM, d), BF16), pltpu.VMEM((TM, d), F32)],
        compiler_params=_cparams("parallel", "arbitrary"),
        name="mlp",
    )(xg, mod, norm_w.reshape(1, d), w1, w2)


def _final_kernel(x_ref, w_ref, o_ref):
    o_ref[...] = _rms(x_ref[...]) * w_ref[...]


def _final_norm(xg, w, nrows):
    d = xg.shape[1]
    return pl.pallas_call(
        _final_kernel,
        grid=(nrows // TM,),
        in_specs=[pl.BlockSpec((TM, d), lambda i: (i, 0)), pl.BlockSpec((1, d), lambda i: (0, 0))],
        out_specs=pl.BlockSpec((TM, d), lambda i: (i, 0)),
        out_shape=jax.ShapeDtypeStruct((nrows, d), F32),
        compiler_params=_cparams("parallel"),
        name="final_norm",
    )(xg, w.reshape(1, d))


def _regroup_w_in(w_in):
    nl, d, _ = w_in.shape
    o_ml = Q_LORA + KV_LORA + QK_ROPE
    o_mlg = o_ml + P_ML
    o_gd = o_mlg + 4 * ML_HEADS
    o_gdg = o_gd + P_GD
    zeros = lambda n: jnp.zeros((nl, d, n), w_in.dtype)
    return jnp.concatenate([
        w_in[..., :o_ml], zeros(P_MLA - o_ml),
        w_in[..., o_ml:o_mlg],
        w_in[..., o_gd:o_gdg],
        w_in[..., o_mlg:o_gd], w_in[..., o_gdg:], zeros(P_GATE - 4 * ML_HEADS - 4 * GD_HEADS),
    ], axis=-1).astype(BF16)


def _regroup_heads(w, first):
    nl, kdim, _ = w.shape
    w4 = w.reshape(nl, kdim, MLA_HEADS, -1)
    return jnp.concatenate([w4[..., :first].reshape(nl, kdim, -1), w4[..., first:].reshape(nl, kdim, -1)],
                           axis=-1).astype(BF16)


def _rope_tables(seq):
    half = QK_ROPE // 2
    t = jnp.arange(seq, dtype=jnp.int32)
    inv = ROPE_BASE ** (-jnp.arange(0, half, 2, dtype=F32) / half)
    ang_r = (t // GRID_W).astype(F32)[:, None] * inv
    ang_c = (t % GRID_W).astype(F32)[:, None] * inv
    cos = jnp.concatenate([jnp.cos(ang_r)] * 2 + [jnp.cos(ang_c)] * 2, axis=-1)
    sin = jnp.concatenate([-jnp.sin(ang_r), jnp.sin(ang_r), -jnp.sin(ang_c), jnp.sin(ang_c)], axis=-1)
    cos = jnp.concatenate([cos, jnp.ones((TM, QK_ROPE), F32)], axis=0)
    sin = jnp.concatenate([sin, jnp.zeros((TM, QK_ROPE), F32)], axis=0)
    return jnp.tile(cos, (1, MLA_HEADS)), jnp.tile(sin, (1, MLA_HEADS))


def _gate_params(ml_gate_bias, gd_a_log, gd_dt_bias):
    nl = ml_gate_bias.shape[0]
    zh = jnp.zeros((nl, GD_HEADS), F32)
    bias = jnp.concatenate([ml_gate_bias, zh, gd_dt_bias[:, 0], zh, gd_dt_bias[:, 1]], axis=-1)
    alog = jnp.concatenate([jnp.zeros((nl, 4 * ML_HEADS), F32), zh, gd_a_log[:, 0], zh, gd_a_log[:, 1]], axis=-1)
    par = jnp.stack([bias, alog], axis=1)
    return jnp.pad(par, ((0, 0), (0, 6), (0, P_GATE - par.shape[-1])))


def kernel(x, c, ctx, c_ctx, w_ada, b_ada, norm1, norm2, w_in, mla_q_norm, mla_w_uq, mla_kv_norm, mla_w_ukv, mla_out_norm, ml_gate_bias, ml_out_norm, gd_conv, gd_a_log, gd_dt_bias, gd_out_norm, w_out, w_mlp1, w_mlp2, final_norm):
    batch, seq, d = x.shape
    nctx = ctx.shape[1]
    depth = w_ada.shape[0]
    n_lat = batch * seq
    n_all = n_lat + batch * nctx
    assert seq % TM == 0 and (batch * nctx) % TM == 0 and nctx % TB == 0 and TM % nctx == 0

    xg = jnp.concatenate([x.reshape(n_lat, d), ctx.reshape(batch * nctx, d)], axis=0)
    mod_all = _ada(jnp.concatenate([c, c_ctx[None]], axis=0), w_ada, b_ada).reshape(depth, batch + 1, 6, d)
    w_in_g = _regroup_w_in(w_in)
    wuq_g = _regroup_heads(mla_w_uq, QK_NOPE)
    wukv_g = _regroup_heads(mla_w_ukv, QK_NOPE)
    w_out_b = w_out.astype(BF16)
    w1_b = w_mlp1.astype(BF16)
    w2_b = w_mlp2.astype(BF16)
    cos_t, sin_t = _rope_tables(seq)
    gate_par = _gate_params(ml_gate_bias, gd_a_log, gd_dt_bias)

    for l in range(depth):
        last = l == depth - 1
        mod = mod_all[l]
        p_mla, p_ml, p_gd, p_gate = _inproj(xg, mod, norm1[l], w_in_g[l], seq, batch)
        q, k, v = _mla_qkv_call(p_mla, mla_q_norm[l], mla_kv_norm[l], wuq_g[l], wukv_g[l], cos_t, sin_t, seq, n_lat)
        a = _attention(q, k, v, batch, seq, nctx)
        gcol, grow = _gates(p_gate, gate_par[l])
        hf, hb = _mlstm(p_ml, gcol, grow, batch, seq, nctx)
        gd_qkv = _gdn_prep(p_gd, gd_conv[l], batch, seq, nctx)
        gf, gb = _gdn(gd_qkv, gcol, grow, batch, seq, nctx)
        nrows = n_lat if last else n_all
        xg = _merge(xg, mod, a, hf, hb, p_ml, gf, gb, p_gd, mla_out_norm[l], ml_out_norm[l], gd_out_norm[l],
                    w_out_b[l], seq, batch, nrows)
        xg = _mlp(xg, mod, norm2[l], w1_b[l], w2_b[l], seq, batch, nrows)
    return _final_norm(xg, final_norm, n_lat).reshape(batch, seq, d)
```

```python
import functools

import jax
import jax.numpy as jnp
from jax import lax
from jax.experimental import pallas as pl
from jax.experimental.pallas import tpu as pltpu

F32 = jnp.float32
BF16 = jnp.bfloat16

EPS = 1e-6
LOG2_E = 1.4426950408889634
M_INIT = -1e30
GRID_W = 64
ROPE_BASE = 10000.0

MLA_HEADS = 8
Q_LORA = 512
KV_LORA = 256
QK_NOPE = 128
QK_ROPE = 64
V_HEAD = 128
QK_DIM = QK_NOPE + QK_ROPE
ML_HEADS = 4
ML_DK = 64
ML_DV = 128
GD_HEADS = 4
GD_DK = 128
GD_DV = 128
CHUNK = 64
CONV_W = 5

MLA_W = MLA_HEADS * V_HEAD
ML_W = ML_HEADS * ML_DV
GD_W = GD_HEADS * GD_DV
MIX_W = MLA_W + ML_W + GD_W

P_MLA = 896
P_ML = 1536
P_GD = 2048
P_GATE = 128
P_ALL = P_MLA + P_ML + P_GD + P_GATE

TM = 512
MLP_TF = 1024
TB = 128
HALO = 16
LANES = 128

VMEM_LIMIT_V7X = 56 * 1024 * 1024

NT_DIMS = (((1,), (1,)), ((), ()))
TN_DIMS = (((0,), (0,)), ((), ()))
NN_DIMS = (((1,), (0,)), ((), ()))


def _cparams(*sem):
    return pltpu.CompilerParams(dimension_semantics=sem, vmem_limit_bytes=VMEM_LIMIT_V7X)


def _sigmoid(x):
    return 1.0 / (1.0 + jnp.exp(-x))


def _rms(x):
    return x * lax.rsqrt(jnp.mean(x * x, axis=-1, keepdims=True) + EPS)


def _norm_mod(x, norm_w, shift, scale):
    return _rms(x) * norm_w * (1.0 + scale) + shift


def _mod_row(i, seq, batch):
    return jnp.minimum((i * TM) // seq, batch)


def _ada_kernel(cb_ref, w_ref, b_ref, o_ref):
    rows = cb_ref.shape[0]
    tn = w_ref.shape[-1]
    for r in range(rows):
        cb = cb_ref[r]
        s = cb * _sigmoid(cb)
        for j in range(tn // LANES):
            cols = slice(j * LANES, (j + 1) * LANES)
            acc = jnp.sum(w_ref[:, cols] * s, axis=0, keepdims=True)
            o_ref[r:r + 1, cols] = acc + b_ref[:, cols]


def _ada(cvec, w_ada, b_ada):
    nl, d, n6 = w_ada.shape
    r = cvec.shape[0]
    tn = 512
    cb = jnp.broadcast_to(cvec[:, :, None], (r, d, LANES))
    return pl.pallas_call(
        _ada_kernel,
        grid=(nl, n6 // tn),
        in_specs=[
            pl.BlockSpec((r, d, LANES), lambda l, j: (0, 0, 0)),
            pl.BlockSpec((None, d, tn), lambda l, j: (l, 0, j)),
            pl.BlockSpec((None, 1, tn), lambda l, j: (l, 0, j)),
        ],
        out_specs=pl.BlockSpec((None, r, tn), lambda l, j: (l, 0, j)),
        out_shape=jax.ShapeDtypeStruct((nl, r, n6), F32),
        compiler_params=_cparams("parallel", "parallel"),
        name="ada",
    )(cb, w_ada, b_ada.reshape(nl, 1, n6))


def _inproj_kernel(x_ref, mod_ref, nw_ref, w_ref, omla_ref, oml_ref, ogd_ref, og_ref):
    h = _norm_mod(x_ref[...], nw_ref[...], mod_ref[0:1, :], mod_ref[1:2, :]).astype(BF16)
    c0 = 0
    for o_ref in (omla_ref, oml_ref, ogd_ref, og_ref):
        n = o_ref.shape[-1]
        o_ref[...] = jnp.dot(h, w_ref[:, c0:c0 + n], preferred_element_type=F32).astype(o_ref.dtype)
        c0 += n


def _inproj(l, xg, mod, norm_w, w, seq, batch):
    nt, d = xg.shape
    row = functools.partial(_mod_row, seq=seq, batch=batch)
    widths = (P_MLA, P_ML, P_GD, P_GATE)
    dtypes = (BF16, BF16, BF16, F32)
    return pl.pallas_call(
        _inproj_kernel,
        grid=(nt // TM,),
        in_specs=[
            pl.BlockSpec((TM, d), lambda i: (i, 0)),
            pl.BlockSpec((None, None, 6, d), lambda i: (l, row(i), 0, 0)),
            pl.BlockSpec((None, 1, d), lambda i: (l, 0, 0)),
            pl.BlockSpec((None, d, P_ALL), lambda i: (l, 0, 0), pipeline_mode=pl.Buffered(1)),
        ],
        out_specs=[pl.BlockSpec((TM, n), lambda i: (i, 0)) for n in widths],
        out_shape=[jax.ShapeDtypeStruct((nt, n), dt) for n, dt in zip(widths, dtypes)],
        compiler_params=_cparams("parallel"),
        name="inproj",
    )(xg, mod, norm_w[:, None, :], w)


def _swap16(x):
    n = x.shape[-1]
    lane = lax.broadcasted_iota(jnp.int32, x.shape, 1)
    return jnp.where(lane % 32 < 16, pltpu.roll(x, n - 16, 1), pltpu.roll(x, 16, 1))


def _mla_qkv_kernel(p_ref, qn_ref, kvn_ref, wuq_ref, wukv_ref, cos_ref, sin_ref, q_ref, k_ref, v_ref):
    p = p_ref[...].astype(F32)
    cq = p[:, :Q_LORA]
    ckv = p[:, Q_LORA:Q_LORA + KV_LORA]
    kpe = p[:, Q_LORA + KV_LORA:]
    cos = cos_ref[...]
    sin = sin_ref[...]
    qn = (_rms(cq) * qn_ref[...]).astype(BF16)
    q = jnp.dot(qn, wuq_ref[...], preferred_element_type=F32) * (QK_DIM ** -0.5 * LOG2_E)
    nope_w = MLA_HEADS * QK_NOPE
    qpe = q[:, nope_w:]
    qpe = qpe * cos + _swap16(qpe) * sin
    kvn = (_rms(ckv) * kvn_ref[...]).astype(BF16)
    kv = jnp.dot(kvn, wukv_ref[...], preferred_element_type=F32)
    kpe = kpe * cos[:, :LANES] + _swap16(kpe) * sin[:, :LANES]
    kpe = kpe[:, :QK_ROPE]
    ones = jnp.ones((q.shape[0], LANES), BF16)
    for h in range(MLA_HEADS):
        qh = jnp.concatenate([q[:, h * QK_NOPE:(h + 1) * QK_NOPE], qpe[:, h * QK_ROPE:(h + 1) * QK_ROPE]], axis=-1)
        kh = jnp.concatenate([kv[:, h * QK_NOPE:(h + 1) * QK_NOPE], kpe], axis=-1)
        q_ref[h] = qh.astype(BF16)
        k_ref[h] = kh.astype(BF16)
        v_ref[h, :, :V_HEAD] = kv[:, nope_w + h * V_HEAD:nope_w + (h + 1) * V_HEAD].astype(BF16)
        v_ref[h, :, V_HEAD:] = ones


def _mla_qkv_call(l, p_mla, q_norm, kv_norm, wuq, wukv, cos_t, sin_t, seq, n_lat_rows):
    nt = p_mla.shape[0]
    lat_tiles = seq // TM
    nl_tiles = n_lat_rows // TM
    pe_w = MLA_HEADS * QK_ROPE
    v1_w = V_HEAD + LANES

    def rope_block(i):
        return (jnp.where(i < nl_tiles, i % lat_tiles, lat_tiles), 0)

    return pl.pallas_call(
        _mla_qkv_kernel,
        grid=(nt // TM,),
        in_specs=[
            pl.BlockSpec((TM, P_MLA), lambda i: (i, 0)),
            pl.BlockSpec((None, 1, Q_LORA), lambda i: (l, 0, 0)),
            pl.BlockSpec((None, 1, KV_LORA), lambda i: (l, 0, 0)),
            pl.BlockSpec((None,) + wuq.shape[1:], lambda i: (l, 0, 0)),
            pl.BlockSpec((None,) + wukv.shape[1:], lambda i: (l, 0, 0)),
            pl.BlockSpec((TM, pe_w), rope_block),
            pl.BlockSpec((TM, pe_w), rope_block),
        ],
        out_specs=[
            pl.BlockSpec((MLA_HEADS, TM, QK_DIM), lambda i: (0, i, 0)),
            pl.BlockSpec((MLA_HEADS, TM, QK_DIM), lambda i: (0, i, 0)),
            pl.BlockSpec((MLA_HEADS, TM, v1_w), lambda i: (0, i, 0)),
        ],
        out_shape=[
            jax.ShapeDtypeStruct((MLA_HEADS, nt, QK_DIM), BF16),
            jax.ShapeDtypeStruct((MLA_HEADS, nt, QK_DIM), BF16),
            jax.ShapeDtypeStruct((MLA_HEADS, nt, v1_w), BF16),
        ],
        compiler_params=_cparams("parallel"),
        name="mla_qkv",
    )(p_mla, q_norm[:, None, :], kv_norm[:, None, :], wuq, wukv, cos_t, sin_t)


KV_CHUNK = 512


def _attn_kernel(q_ref, *refs):
    o_ref = refs[-1]
    q = q_ref[...]
    chunks = []
    for k_ref, v_ref in zip(refs[0:-1:2], refs[1:-1:2]):
        n = k_ref.shape[0]
        step = min(KV_CHUNK, n)
        chunks += [(k_ref, v_ref, c0, step) for c0 in range(0, n, step)]
    scores = lambda c: lax.dot_general(q, c[0][c[2]:c[2] + c[3], :], NT_DIMS, preferred_element_type=F32)
    m = jnp.full((q.shape[0], 1), -jnp.inf, F32)
    acc = jnp.zeros((q.shape[0], V_HEAD + LANES), F32)
    s_next = scores(chunks[0])
    for idx, c in enumerate(chunks):
        s = s_next
        if idx + 1 < len(chunks):
            s_next = scores(chunks[idx + 1])
        m_new = jnp.maximum(m, jnp.max(s, axis=-1, keepdims=True))
        p = jnp.exp2(s - m_new).astype(BF16)
        acc = jnp.exp2(m - m_new) * acc + jnp.dot(p, c[1][c[2]:c[2] + c[3], :], preferred_element_type=F32)
        m = m_new
    o_ref[...] = (acc[:, :V_HEAD] / acc[:, V_HEAD:]).astype(o_ref.dtype)


def _attention(q, k, v1, batch, seq, nctx):
    tq = TM
    nq = seq // tq
    ctx0 = batch * seq // nctx
    v1_w = v1.shape[-1]
    a_lat = pl.pallas_call(
        _attn_kernel,
        grid=(batch, MLA_HEADS, nq),
        in_specs=[
            pl.BlockSpec((None, tq, QK_DIM), lambda b, h, i: (h, b * nq + i, 0)),
            pl.BlockSpec((None, nctx, QK_DIM), lambda b, h, i: (h, ctx0 + b, 0)),
            pl.BlockSpec((None, nctx, v1_w), lambda b, h, i: (h, ctx0 + b, 0)),
            pl.BlockSpec((None, seq, QK_DIM), lambda b, h, i: (h, b, 0)),
            pl.BlockSpec((None, seq, v1_w), lambda b, h, i: (h, b, 0)),
        ],
        out_specs=pl.BlockSpec((tq, V_HEAD), lambda b, h, i: (b * nq + i, h)),
        out_shape=jax.ShapeDtypeStruct((batch * seq, MLA_W), BF16),
        compiler_params=_cparams("parallel", "parallel", "arbitrary"),
        name="attn_lat",
    )(q, k, v1, k, v1)
    a_ctx = pl.pallas_call(
        _attn_kernel,
        grid=(batch, MLA_HEADS),
        in_specs=[
            pl.BlockSpec((None, nctx, QK_DIM), lambda b, h: (h, ctx0 + b, 0)),
            pl.BlockSpec((None, nctx, QK_DIM), lambda b, h: (h, ctx0 + b, 0)),
            pl.BlockSpec((None, nctx, v1_w), lambda b, h: (h, ctx0 + b, 0)),
        ],
        out_specs=pl.BlockSpec((nctx, V_HEAD), lambda b, h: (b, h)),
        out_shape=jax.ShapeDtypeStruct((batch * nctx, MLA_W), BF16),
        compiler_params=_cparams("parallel", "parallel"),
        name="attn_ctx",
    )(q, k, v1)
    return a_lat, a_ctx


def _gates_kernel(g_ref, par_ref, col_ref, row_ref):
    tb = g_ref.shape[0]
    z = g_ref[...] + par_ref[0:1, :]
    neg_a = -jnp.exp(par_ref[1:2, :])
    lane = lax.broadcasted_iota(jnp.int32, z.shape, 1)
    kind = (lane // 4) % 4
    is_ml = lane < 16
    is_cum = (kind % 2 == 1) & (lane < 32)
    is_bwd = kind >= 2
    soft = jnp.log(1.0 + jnp.exp(-jnp.abs(z)))
    log_sig = jnp.minimum(z, 0.0) - soft
    softplus = jnp.maximum(z, 0.0) + soft
    pre = jnp.where(is_ml, jnp.where(is_cum, log_sig, z), jnp.where(is_cum, neg_a * softplus, _sigmoid(z)))
    r = lax.broadcasted_iota(jnp.int32, (tb, tb), 0)
    c = lax.broadcasted_iota(jnp.int32, (tb, tb), 1)
    same = (r // CHUNK) == (c // CHUNK)
    lower = jnp.where(same & (c <= r), 1.0, 0.0).astype(F32)
    upper = jnp.where(same & (c >= r), 1.0, 0.0).astype(F32)
    cum_f = jnp.dot(lower, pre, preferred_element_type=F32, precision=lax.Precision.HIGHEST)
    cum_b = jnp.dot(upper, pre, preferred_element_type=F32, precision=lax.Precision.HIGHEST)
    out = jnp.where(is_cum, jnp.where(is_bwd, cum_b, cum_f), pre)
    col_ref[...] = out
    row_ref[...] = out.T


def _gates(l, g, par):
    nt = g.shape[0]
    return pl.pallas_call(
        _gates_kernel,
        grid=(nt // TB,),
        in_specs=[
            pl.BlockSpec((TB, P_GATE), lambda i: (i, 0)),
            pl.BlockSpec((None, 8, P_GATE), lambda i: (l, 0, 0)),
        ],
        out_specs=[
            pl.BlockSpec((TB, P_GATE), lambda i: (i, 0)),
            pl.BlockSpec((P_GATE, TB), lambda i: (0, i)),
        ],
        out_shape=[
            jax.ShapeDtypeStruct((nt, P_GATE), F32),
            jax.ShapeDtypeStruct((P_GATE, nt), F32),
        ],
        compiler_params=_cparams("parallel"),
        name="gates",
    )(g, par)


def _scan_blocks(batch, seq, nctx):
    ncb = nctx // TB
    nlb = seq // TB
    ctx0 = batch * nlb

    def fwd(b, j):
        return jnp.where(j < ncb, ctx0 + b * ncb + j, b * nlb + (j - ncb))

    def bwd(b, j):
        return jnp.where(j < ncb, ctx0 + b * ncb + (ncb - 1 - j), b * nlb + (nlb - 1 - (j - ncb)))

    return ncb + nlb, fwd, bwd


def _dot(a, b, dims=NN_DIMS):
    return lax.dot_general(a.astype(BF16), b.astype(BF16), dims, preferred_element_type=F32)


def _scan_units(nheads):
    return [(d, h) for d in range(2) for h in range(nheads)]


def _mlstm_kernel(xf_ref, xb_ref, gcf_ref, gcb_ref, grf_ref, grb_ref, hf_ref, hb_ref, c_ref, m_ref):
    @pl.when(pl.program_id(1) == 0)
    def _():
        c_ref[...] = jnp.zeros_like(c_ref)
        m_ref[...] = jnp.full(m_ref.shape, M_INIT, F32)

    nch = TB // CHUNK
    row = lax.broadcasted_iota(jnp.int32, (CHUNK, CHUNK), 0)
    col = lax.broadcasted_iota(jnp.int32, (CHUNK, CHUNK), 1)
    kscale = ML_DK ** -0.5
    qw = ML_HEADS * ML_DK
    refs = ((xf_ref, gcf_ref, grf_ref, hf_ref), (xb_ref, gcb_ref, grb_ref, hb_ref))
    units = _scan_units(ML_HEADS)

    ones = jnp.ones((CHUNK, LANES), BF16)
    loc = {}
    for ci in range(nch):
        for d, h in units:
            x_ref, gc_ref, gr_ref, _ = refs[d]
            ch = ci if d == 0 else nch - 1 - ci
            rs = slice(ch * CHUNK, (ch + 1) * CHUNK)
            gi = d * 8 + h
            gb = d * 8 + 4 + h
            i_rep = jnp.broadcast_to(gc_ref[rs, gi:gi + 1], (CHUNK, LANES))
            b_rep = jnp.broadcast_to(gc_ref[rs, gb:gb + 1], (CHUNK, LANES))
            i_row = gr_ref[gi:gi + 1, rs]
            b_row = gr_ref[gb:gb + 1, rs]
            b_end = b_rep[CHUNK - 1:CHUNK, :] if d == 0 else b_rep[0:1, :]
            mask = (col <= row) if d == 0 else (col >= row)
            log_d = jnp.where(mask, b_rep[:, :CHUNK] - b_row + i_row, -jnp.inf)
            log_w = b_end - b_rep + i_rep
            loc[ci, d, h] = dict(
                rs=rs, b_rep=b_rep, b_end=b_end, log_d=log_d, log_w=log_w,
                q=x_ref[rs, h * ML_DK:(h + 1) * ML_DK],
                k=x_ref[rs, qw + h * ML_DK:qw + (h + 1) * ML_DK],
                v1=jnp.concatenate([x_ref[rs, 2 * qw + h * ML_DV:2 * qw + (h + 1) * ML_DV], ones], axis=-1),
                row_max=jnp.broadcast_to(jnp.max(log_d, axis=-1, keepdims=True), (CHUNK, LANES)),
                w_max=jnp.max(log_w, axis=0, keepdims=True))
    for u in loc.values():
        u["qk"] = _dot(u["q"], u["k"], NT_DIMS)

    state = {(d, h): (c_ref[d * ML_HEADS + h], m_ref[d * ML_HEADS + h]) for d, h in units}
    for ci in range(nch):
        cur = [(dh, loc[(ci,) + dh], state[dh]) for dh in units]
        for _, u, (cn, m) in cur:
            log_inter = u["b_rep"] + m
            m_t = jnp.maximum(log_inter, u["row_max"])
            u["w_inter"] = jnp.exp(log_inter - m_t)
            u["floor"] = jnp.exp(-m_t)
            u["s"] = u["qk"] * jnp.exp(u["log_d"] - m_t[:, :CHUNK]) * kscale
        for _, u, (cn, m) in cur:
            u["qc"] = _dot(u["q"], cn)
            u["sv"] = _dot(u["s"], u["v1"])
        for (d, h), u, (cn, m) in cur:
            num = u["w_inter"] * u["qc"][:, :ML_DV] + u["sv"][:, :ML_DV]
            den = u["w_inter"] * u["qc"][:, ML_DV:] + u["sv"][:, ML_DV:]
            refs[d][3][u["rs"], h * ML_DV:(h + 1) * ML_DV] = num / jnp.maximum(jnp.abs(den), u["floor"])
            m_new = jnp.maximum(u["b_end"] + m, u["w_max"])
            u["decay"] = jnp.exp(u["b_end"] + m - m_new)
            u["m_new"] = m_new
            u["kw"] = u["k"].astype(F32) * (jnp.exp(u["log_w"] - m_new)[:, :ML_DK] * kscale)
        for _, u, _st in cur:
            u["upd"] = _dot(u["kw"], u["v1"], TN_DIMS)
        for dh, u, (cn, m) in cur:
            state[dh] = (jnp.concatenate([u["decay"], u["decay"]], axis=-1) * cn + u["upd"], u["m_new"])
    for d, h in units:
        st = d * ML_HEADS + h
        c_ref[st], m_ref[st] = state[(d, h)]


def _mlstm(p_ml, gcol, grow, batch, seq, nctx):
    nt = p_ml.shape[0]
    nsteps, fwd, bwd = _scan_blocks(batch, seq, nctx)
    xw = 2 * ML_HEADS * ML_DK + ML_W
    nst = 2 * ML_HEADS
    return pl.pallas_call(
        _mlstm_kernel,
        grid=(batch, nsteps),
        in_specs=[
            pl.BlockSpec((TB, xw), lambda b, j: (fwd(b, j), 0)),
            pl.BlockSpec((TB, xw), lambda b, j: (bwd(b, j), 0)),
            pl.BlockSpec((TB, P_GATE), lambda b, j: (fwd(b, j), 0)),
            pl.BlockSpec((TB, P_GATE), lambda b, j: (bwd(b, j), 0)),
            pl.BlockSpec((P_GATE, TB), lambda b, j: (0, fwd(b, j))),
            pl.BlockSpec((P_GATE, TB), lambda b, j: (0, bwd(b, j))),
        ],
        out_specs=[
            pl.BlockSpec((TB, ML_W), lambda b, j: (fwd(b, j), 0)),
            pl.BlockSpec((TB, ML_W), lambda b, j: (bwd(b, j), 0)),
        ],
        out_shape=[jax.ShapeDtypeStruct((nt, ML_W), F32)] * 2,
        scratch_shapes=[
            pltpu.VMEM((nst, ML_DK, ML_DV + LANES), F32),
            pltpu.VMEM((nst, 1, LANES), F32),
        ],
        compiler_params=_cparams("parallel", "arbitrary"),
        name="mlstm",
    )(p_ml, p_ml, gcol, gcol, grow, grow)


def _gdn_prep_kernel(x_ref, prev_ref, next_ref, w_ref, o_ref, buf_ref, *, seq_blocks, lat_blocks):
    i = pl.program_id(0)
    in_lat = i < lat_blocks
    first = jnp.where(in_lat, i % seq_blocks == 0, True)
    last = jnp.where(in_lat, i % seq_blocks == seq_blocks - 1, True)
    tb = x_ref.shape[0]
    prev = prev_ref[...].astype(F32)[HALO - 8:, :]
    nxt = next_ref[...].astype(F32)[:8, :]
    buf_ref[0:8, :] = jnp.where(first, 0.0, prev)
    buf_ref[8:8 + tb, :] = x_ref[...].astype(F32)
    buf_ref[8 + tb:16 + tb, :] = jnp.where(last, 0.0, nxt)
    half = CONV_W // 2
    y = None
    for j in range(CONV_W):
        term = buf_ref[8 - half + j:8 - half + j + tb, :] * w_ref[j:j + 1, :]
        y = term if y is None else y + term
    y = y * _sigmoid(y)
    nk = GD_HEADS * GD_DK
    for h in range(2 * GD_HEADS):
        cols = slice(h * GD_DK, (h + 1) * GD_DK)
        seg = y[:, cols]
        seg = seg * lax.rsqrt(jnp.sum(seg * seg, axis=-1, keepdims=True) + EPS)
        if h < GD_HEADS:
            seg = seg * (GD_DK ** -0.5)
        o_ref[:, cols] = seg
    o_ref[:, 2 * nk:] = y[:, 2 * nk:]


def _gdn_prep(l, p_gd, conv_w, batch, seq, nctx):
    nt = p_gd.shape[0]
    cw = 2 * GD_HEADS * GD_DK + GD_W
    tb = nctx
    hb = tb // HALO
    nblk = nt // tb
    kern = functools.partial(_gdn_prep_kernel, seq_blocks=seq // tb, lat_blocks=batch * seq // tb)
    return pl.pallas_call(
        kern,
        grid=(nblk,),
        in_specs=[
            pl.BlockSpec((tb, cw), lambda i: (i, 0)),
            pl.BlockSpec((HALO, cw), lambda i: (jnp.maximum(i * hb - 1, 0), 0)),
            pl.BlockSpec((HALO, cw), lambda i: (jnp.minimum((i + 1) * hb, nblk * hb - 1), 0)),
            pl.BlockSpec((None, CONV_W, cw), lambda i: (l, 0, 0)),
        ],
        out_specs=pl.BlockSpec((tb, cw), lambda i: (i, 0)),
        out_shape=jax.ShapeDtypeStruct((nt, cw), F32),
        scratch_shapes=[pltpu.VMEM((tb + 16, cw), F32)],
        compiler_params=_cparams("parallel"),
        name="gdn_prep",
    )(p_gd, p_gd, p_gd, conv_w)


INV_BLOCK = 16


def _unit_triangular_inverses(mats):
    n = mats[0].shape[0]
    row = lax.broadcasted_iota(jnp.int32, (n, n), 0)
    col = lax.broadcasted_iota(jnp.int32, (n, n), 1)
    eye = jnp.where(row == col, 1.0, 0.0)
    same_block = row // INV_BLOCK == col // INV_BLOCK
    ps = [jnp.where(same_block, a, 0.0) for a in mats]
    xs = [eye - p for p in ps]
    for _ in range((INV_BLOCK - 1).bit_length() - 1):
        ps = [_dot(p, p) for p in ps]
        xs = [x + xp for x, xp in zip(xs, [_dot(x, p) for x, p in zip(xs, ps)])]
    size = INV_BLOCK
    while size < n:
        sel = (row // (2 * size) == col // (2 * size)) & (row // size != col // size)
        ts = [_dot(x, jnp.where(sel, a, 0.0)) for x, a in zip(xs, mats)]
        xs = [x - tx for x, tx in zip(xs, [_dot(t, x) for t, x in zip(ts, xs)])]
        size *= 2
    return xs


def _gdn_kernel(xf_ref, xb_ref, gcf_ref, gcb_ref, grf_ref, grb_ref, of_ref, ob_ref, s_ref):
    @pl.when(pl.program_id(1) == 0)
    def _():
        s_ref[...] = jnp.zeros_like(s_ref)

    nch = TB // CHUNK
    row = lax.broadcasted_iota(jnp.int32, (CHUNK, CHUNK), 0)
    col = lax.broadcasted_iota(jnp.int32, (CHUNK, CHUNK), 1)
    nk = GD_HEADS * GD_DK
    refs = ((xf_ref, gcf_ref, grf_ref, of_ref), (xb_ref, gcb_ref, grb_ref, ob_ref))
    units = _scan_units(GD_HEADS)

    loc = {}
    for ci in range(nch):
        for d, h in units:
            x_ref, gc_ref, gr_ref, _ = refs[d]
            ch = ci if d == 0 else nch - 1 - ci
            rs = slice(ch * CHUNK, (ch + 1) * CHUNK)
            cb = 16 + d * 8 + h
            cg = 16 + d * 8 + 4 + h
            beta = jnp.broadcast_to(gc_ref[rs, cb:cb + 1], (CHUNK, LANES))
            g_col = jnp.broadcast_to(gc_ref[rs, cg:cg + 1], (CHUNK, LANES))
            g_row = gr_ref[cg:cg + 1, rs]
            g_end = g_col[CHUNK - 1:CHUNK, :] if d == 0 else g_col[0:1, :]
            incl = (col <= row) if d == 0 else (col >= row)
            q = x_ref[rs, h * GD_DK:(h + 1) * GD_DK]
            k = x_ref[rs, nk + h * GD_DK:nk + (h + 1) * GD_DK]
            v = x_ref[rs, 2 * nk + h * GD_DV:2 * nk + (h + 1) * GD_DV]
            kb = k * beta
            e_col = jnp.exp(g_col)
            loc[ci, d, h] = dict(
                rs=rs, k=k, kb=kb, q=q,
                strict=(col < row) if d == 0 else (col > row),
                decay=jnp.exp(jnp.where(incl, g_col[:, :CHUNK] - g_row, -jnp.inf)),
                rhs=jnp.concatenate([v * beta, kb * e_col], axis=-1),
                qg=q * e_col, kd=k * jnp.exp(g_end - g_col), e_end=jnp.exp(g_end))
    us = list(loc.values())
    for u in us:
        u["kk"] = _dot(u["kb"], u["k"], NT_DIMS)
        u["attn"] = _dot(u["q"], u["k"], NT_DIMS) * u["decay"]
    tinvs = _unit_triangular_inverses([jnp.where(u["strict"], u["kk"] * u["decay"], 0.0) for u in us])
    for u, tinv in zip(us, tinvs):
        u["uw"] = _dot(tinv, u["rhs"])

    state = {(d, h): s_ref[d * GD_HEADS + h] for d, h in units}
    for ci in range(nch):
        cur = [(dh, loc[(ci,) + dh]) for dh in units]
        for dh, u in cur:
            u["ws"] = _dot(u["uw"][:, GD_DV:], state[dh])
            u["qs"] = _dot(u["qg"], state[dh])
        for dh, u in cur:
            u["v_new"] = u["uw"][:, :GD_DV] - u["ws"]
        for dh, u in cur:
            u["av"] = _dot(u["attn"], u["v_new"])
            u["upd"] = _dot(u["kd"], u["v_new"], TN_DIMS)
        for (d, h), u in cur:
            refs[d][3][u["rs"], h * GD_DV:(h + 1) * GD_DV] = u["qs"] + u["av"]
            state[(d, h)] = state[(d, h)] * u["e_end"] + u["upd"]
    for d, h in units:
        s_ref[d * GD_HEADS + h] = state[(d, h)]


def _gdn(qkv, gcol, grow, batch, seq, nctx):
    nt = qkv.shape[0]
    nsteps, fwd, bwd = _scan_blocks(batch, seq, nctx)
    cw = qkv.shape[1]
    return pl.pallas_call(
        _gdn_kernel,
        grid=(batch, nsteps),
        in_specs=[
            pl.BlockSpec((TB, cw), lambda b, j: (fwd(b, j), 0)),
            pl.BlockSpec((TB, cw), lambda b, j: (bwd(b, j), 0)),
            pl.BlockSpec((TB, P_GATE), lambda b, j: (fwd(b, j), 0)),
            pl.BlockSpec((TB, P_GATE), lambda b, j: (bwd(b, j), 0)),
            pl.BlockSpec((P_GATE, TB), lambda b, j: (0, fwd(b, j))),
            pl.BlockSpec((P_GATE, TB), lambda b, j: (0, bwd(b, j))),
        ],
        out_specs=[
            pl.BlockSpec((TB, GD_W), lambda b, j: (fwd(b, j), 0)),
            pl.BlockSpec((TB, GD_W), lambda b, j: (bwd(b, j), 0)),
        ],
        out_shape=[jax.ShapeDtypeStruct((nt, GD_W), F32)] * 2,
        scratch_shapes=[pltpu.VMEM((2 * GD_HEADS, GD_DK, GD_DV), F32)],
        compiler_params=_cparams("parallel", "arbitrary"),
        name="gdn",
    )(qkv, qkv, gcol, gcol, grow, grow)


def _head_rms(y, gain, nheads, width):
    parts = []
    for h in range(nheads):
        seg = y[:, h * width:(h + 1) * width]
        parts.append(_rms(seg))
    return jnp.concatenate(parts, axis=-1) * gain


def _merge_kernel(x_ref, mod_ref, al_ref, ac_ref, hf_ref, hb_ref, o_ref, gf_ref, gb_ref, z_ref,
                  an_ref, mn_ref, gn_ref, w_ref, out_ref, *, lat_tiles):
    a = jnp.where(pl.program_id(0) < lat_tiles, al_ref[...], ac_ref[...]).astype(F32)
    ya = _head_rms(a, an_ref[...], MLA_HEADS, V_HEAD)
    ym = _head_rms(hf_ref[...] + hb_ref[...], mn_ref[...], ML_HEADS, ML_DV) * _sigmoid(o_ref[...].astype(F32))
    z = z_ref[...].astype(F32)
    yg = _head_rms(gf_ref[...] + gb_ref[...], gn_ref[...], GD_HEADS, GD_DV) * (z * _sigmoid(z))
    y = jnp.dot(ya.astype(BF16), w_ref[0:MLA_W, :], preferred_element_type=F32)
    y = y + jnp.dot(ym.astype(BF16), w_ref[MLA_W:MLA_W + ML_W, :], preferred_element_type=F32)
    y = y + jnp.dot(yg.astype(BF16), w_ref[MLA_W + ML_W:, :], preferred_element_type=F32)
    out_ref[...] = x_ref[...] + mod_ref[2:3, :] * y


def _merge(l, xg, mod, a_lat, a_ctx, hf, hb, p_ml, gf, gb, p_gd, mla_norm, ml_norm, gd_norm, w_out, seq, batch, nrows):
    d = xg.shape[1]
    lat_tiles = a_lat.shape[0] // TM
    assert a_ctx.shape[0] == TM
    row = functools.partial(_mod_row, seq=seq, batch=batch)
    tile = lambda w, cb=0: pl.BlockSpec((TM, w), lambda i: (i, cb))
    vec = lambda w: pl.BlockSpec((None, 1, w), lambda i: (l, 0, 0))
    return pl.pallas_call(
        functools.partial(_merge_kernel, lat_tiles=lat_tiles),
        grid=(nrows // TM,),
        in_specs=[
            tile(d),
            pl.BlockSpec((None, None, 6, d), lambda i: (l, row(i), 0, 0)),
            pl.BlockSpec((TM, MLA_W), lambda i: (jnp.minimum(i, lat_tiles - 1), 0)),
            pl.BlockSpec((TM, MLA_W), lambda i: (0, 0)),
            tile(ML_W), tile(ML_W), tile(ML_W, 2),
            tile(GD_W), tile(GD_W), tile(GD_W, 3),
            vec(MLA_W), vec(ML_W), vec(GD_W),
            pl.BlockSpec((None, MIX_W, d), lambda i: (l, 0, 0)),
        ],
        out_specs=tile(d),
        out_shape=jax.ShapeDtypeStruct((nrows, d), F32),
        compiler_params=_cparams("parallel"),
        name="merge",
    )(xg, mod, a_lat, a_ctx, hf, hb, p_ml, gf, gb, p_gd, mla_norm[:, None, :], ml_norm[:, None, :],
      jnp.tile(gd_norm, (1, GD_HEADS))[:, None, :], w_out)


def _mlp_kernel(x_ref, mod_ref, nw_ref, w1_ref, w2_ref, out_ref, h_ref, acc_ref):
    j = pl.program_id(1)

    @pl.when(j == 0)
    def _():
        h_ref[...] = _norm_mod(x_ref[...], nw_ref[...], mod_ref[3:4, :], mod_ref[4:5, :]).astype(BF16)
        acc_ref[...] = jnp.zeros_like(acc_ref)

    a = jnp.maximum(jnp.dot(h_ref[...], w1_ref[...], preferred_element_type=F32), 0.0)
    acc_ref[...] += jnp.dot((a * a).astype(BF16), w2_ref[...], preferred_element_type=F32)

    @pl.when(j == pl.num_programs(1) - 1)
    def _():
        out_ref[...] = x_ref[...] + mod_ref[5:6, :] * acc_ref[...]


def _mlp(l, xg, mod, norm_w, w1, w2, seq, batch, nrows):
    d = xg.shape[1]
    f = w1.shape[-1]
    tf = min(MLP_TF, f)
    row = functools.partial(_mod_row, seq=seq, batch=batch)
    return pl.pallas_call(
        _mlp_kernel,
        grid=(nrows // TM, f // tf),
        in_specs=[
            pl.BlockSpec((TM, d), lambda i, j: (i, 0)),
            pl.BlockSpec((None, None, 6, d), lambda i, j: (l, row(i), 0, 0)),
            pl.BlockSpec((None, 1, d), lambda i, j: (l, 0, 0)),
            pl.BlockSpec((None, d, tf), lambda i, j: (l, 0, j)),
            pl.BlockSpec((None, tf, d), lambda i, j: (l, j, 0)),
        ],
        out_specs=pl.BlockSpec((TM, d), lambda i, j: (i, 0)),
        out_shape=jax.ShapeDtypeStruct((nrows, d), F32),
        scratch_shapes=[pltpu.VMEM((TM, d), BF16), pltpu.VMEM((TM, d), F32)],
        compiler_params=_cparams("parallel", "arbitrary"),
        name="mlp",
    )(xg, mod, norm_w[:, None, :], w1, w2)


def _final_kernel(x_ref, w_ref, o_ref):
    o_ref[...] = _rms(x_ref[...]) * w_ref[...]


def _final_norm(xg, w, nrows):
    d = xg.shape[1]
    return pl.pallas_call(
        _final_kernel,
        grid=(nrows // TM,),
        in_specs=[pl.BlockSpec((TM, d), lambda i: (i, 0)), pl.BlockSpec((1, d), lambda i: (0, 0))],
        out_specs=pl.BlockSpec((TM, d), lambda i: (i, 0)),
        out_shape=jax.ShapeDtypeStruct((nrows, d), F32),
        compiler_params=_cparams("parallel"),
        name="final_norm",
    )(xg, w.reshape(1, d))


def _regroup_w_in(w_in):
    nl, d, _ = w_in.shape
    o_ml = Q_LORA + KV_LORA + QK_ROPE
    o_mlg = o_ml + P_ML
    o_gd = o_mlg + 4 * ML_HEADS
    o_gdg = o_gd + P_GD
    zeros = lambda n: jnp.zeros((nl, d, n), w_in.dtype)
    return jnp.concatenate([
        w_in[..., :o_ml], zeros(P_MLA - o_ml),
        w_in[..., o_ml:o_mlg],
        w_in[..., o_gd:o_gdg],
        w_in[..., o_mlg:o_gd], w_in[..., o_gdg:], zeros(P_GATE - 4 * ML_HEADS - 4 * GD_HEADS),
    ], axis=-1).astype(BF16)


def _regroup_heads(w, first):
    nl, kdim, _ = w.shape
    w4 = w.reshape(nl, kdim, MLA_HEADS, -1)
    return jnp.concatenate([w4[..., :first].reshape(nl, kdim, -1), w4[..., first:].reshape(nl, kdim, -1)],
                           axis=-1).astype(BF16)


def _rope_tables(seq):
    half = QK_ROPE // 2
    t = jnp.arange(seq, dtype=jnp.int32)
    inv = ROPE_BASE ** (-jnp.arange(0, half, 2, dtype=F32) / half)
    ang_r = (t // GRID_W).astype(F32)[:, None] * inv
    ang_c = (t % GRID_W).astype(F32)[:, None] * inv
    cos = jnp.concatenate([jnp.cos(ang_r)] * 2 + [jnp.cos(ang_c)] * 2, axis=-1)
    sin = jnp.concatenate([-jnp.sin(ang_r), jnp.sin(ang_r), -jnp.sin(ang_c), jnp.sin(ang_c)], axis=-1)
    cos = jnp.concatenate([cos, jnp.ones((TM, QK_ROPE), F32)], axis=0)
    sin = jnp.concatenate([sin, jnp.zeros((TM, QK_ROPE), F32)], axis=0)
    return jnp.tile(cos, (1, MLA_HEADS)), jnp.tile(sin, (1, MLA_HEADS))


def _gate_params(ml_gate_bias, gd_a_log, gd_dt_bias):
    nl = ml_gate_bias.shape[0]
    zh = jnp.zeros((nl, GD_HEADS), F32)
    bias = jnp.concatenate([ml_gate_bias, zh, gd_dt_bias[:, 0], zh, gd_dt_bias[:, 1]], axis=-1)
    alog = jnp.concatenate([jnp.zeros((nl, 4 * ML_HEADS), F32), zh, gd_a_log[:, 0], zh, gd_a_log[:, 1]], axis=-1)
    par = jnp.stack([bias, alog], axis=1)
    return jnp.pad(par, ((0, 0), (0, 6), (0, P_GATE - par.shape[-1])))


def kernel(x, c, ctx, c_ctx, w_ada, b_ada, norm1, norm2, w_in, mla_q_norm, mla_w_uq, mla_kv_norm, mla_w_ukv, mla_out_norm, ml_gate_bias, ml_out_norm, gd_conv, gd_a_log, gd_dt_bias, gd_out_norm, w_out, w_mlp1, w_mlp2, final_norm):
    batch, seq, d = x.shape
    nctx = ctx.shape[1]
    depth = w_ada.shape[0]
    n_lat = batch * seq
    n_all = n_lat + batch * nctx
    assert seq % TM == 0 and (batch * nctx) % TM == 0 and nctx % TB == 0 and TM % nctx == 0

    xg = jnp.concatenate([x.reshape(n_lat, d), ctx.reshape(batch * nctx, d)], axis=0)
    mod_all = _ada(jnp.concatenate([c, c_ctx[None]], axis=0), w_ada, b_ada).reshape(depth, batch + 1, 6, d)
    w_in_g = _regroup_w_in(w_in)
    wuq_g = _regroup_heads(mla_w_uq, QK_NOPE)
    wukv_g = _regroup_heads(mla_w_ukv, QK_NOPE)
    w_out_b = w_out.astype(BF16)
    w1_b = w_mlp1.astype(BF16)
    w2_b = w_mlp2.astype(BF16)
    cos_t, sin_t = _rope_tables(seq)
    gate_par = _gate_params(ml_gate_bias, gd_a_log, gd_dt_bias)

    for l in range(depth):
        last = l == depth - 1
        p_mla, p_ml, p_gd, p_gate = _inproj(l, xg, mod_all, norm1, w_in_g, seq, batch)
        q, k, v1 = _mla_qkv_call(l, p_mla, mla_q_norm, mla_kv_norm, wuq_g, wukv_g, cos_t, sin_t, seq, n_lat)
        a_lat, a_ctx = _attention(q, k, v1, batch, seq, nctx)
        gcol, grow = _gates(l, p_gate, gate_par)
        hf, hb = _mlstm(p_ml, gcol, grow, batch, seq, nctx)
        gd_qkv = _gdn_prep(l, p_gd, gd_conv, batch, seq, nctx)
        gf, gb = _gdn(gd_qkv, gcol, grow, batch, seq, nctx)
        nrows = n_lat if last else n_all
        xg = _merge(l, xg, mod_all, a_lat, a_ctx, hf, hb, p_ml, gf, gb, p_gd, mla_out_norm, ml_out_norm,
                    gd_out_norm, w_out_b, seq, batch, nrows)
        xg = _mlp(l, xg, mod_all, norm2, w1_b, w2_b, seq, batch, nrows)
    return _final_norm(xg, final_norm, n_lat).reshape(batch, seq, d)
```

```python
import functools

import jax
import jax.numpy as jnp
from jax import lax
from jax.experimental import pallas as pl
from jax.experimental.pallas import tpu as pltpu

F32 = jnp.float32
BF16 = jnp.bfloat16

EPS = 1e-6
LOG2_E = 1.4426950408889634
M_INIT = -1e30
GRID_W = 64
ROPE_BASE = 10000.0

MLA_HEADS = 8
Q_LORA = 512
KV_LORA = 256
QK_NOPE = 128
QK_ROPE = 64
V_HEAD = 128
QK_DIM = QK_NOPE + QK_ROPE
ML_HEADS = 4
ML_DK = 64
ML_DV = 128
GD_HEADS = 4
GD_DK = 128
GD_DV = 128
CHUNK = 64
CONV_W = 5

MLA_W = MLA_HEADS * V_HEAD
ML_W = ML_HEADS * ML_DV
GD_W = GD_HEADS * GD_DV
MIX_W = MLA_W + ML_W + GD_W

P_MLA = 896
P_ML = 1536
P_GD = 2048
P_GATE = 128
P_ALL = P_MLA + P_ML + P_GD + P_GATE

TM = 512
MLP_TF = 1024
TB = 256
HALO = 16
LANES = 128

VMEM_LIMIT_V7X = 56 * 1024 * 1024

NT_DIMS = (((1,), (1,)), ((), ()))
TN_DIMS = (((0,), (0,)), ((), ()))
NN_DIMS = (((1,), (0,)), ((), ()))


def _cparams(*sem):
    return pltpu.CompilerParams(dimension_semantics=sem, vmem_limit_bytes=VMEM_LIMIT_V7X)


def _sigmoid(x):
    return 1.0 / (1.0 + jnp.exp(-x))


def _rms(x):
    return x * lax.rsqrt(jnp.mean(x * x, axis=-1, keepdims=True) + EPS)


def _norm_mod(x, norm_w, shift, scale):
    return _rms(x) * norm_w * (1.0 + scale) + shift


def _mod_row(i, seq, batch):
    return jnp.minimum((i * TM) // seq, batch)


def _ada_kernel(cb_ref, w_ref, b_ref, o_ref, s_ref):
    @pl.when((pl.program_id(0) == 0) & (pl.program_id(1) == 0))
    def _():
        cb = cb_ref[...]
        s_ref[...] = cb * _sigmoid(cb)

    rows = cb_ref.shape[0]
    tn = w_ref.shape[-1]
    for j in range(tn // LANES):
        cols = slice(j * LANES, (j + 1) * LANES)
        w = w_ref[:, cols]
        for r in range(rows):
            o_ref[r:r + 1, cols] = jnp.sum(w * s_ref[r], axis=0, keepdims=True) + b_ref[:, cols]


def _ada(cvec, w_ada, b_ada):
    nl, d, n6 = w_ada.shape
    r = cvec.shape[0]
    tn = 512
    cb = jnp.broadcast_to(cvec[:, :, None], (r, d, LANES))
    return pl.pallas_call(
        _ada_kernel,
        grid=(nl, n6 // tn),
        in_specs=[
            pl.BlockSpec((r, d, LANES), lambda l, j: (0, 0, 0)),
            pl.BlockSpec((None, d, tn), lambda l, j: (l, 0, j)),
            pl.BlockSpec((None, 1, tn), lambda l, j: (l, 0, j)),
        ],
        out_specs=pl.BlockSpec((None, r, tn), lambda l, j: (l, 0, j)),
        out_shape=jax.ShapeDtypeStruct((nl, r, n6), F32),
        scratch_shapes=[pltpu.VMEM((r, d, LANES), F32)],
        compiler_params=_cparams("arbitrary", "arbitrary"),
        name="ada",
    )(cb, w_ada, b_ada.reshape(nl, 1, n6))


def _token_specs(xs, width):
    if len(xs) == 1:
        return [pl.BlockSpec((TM, width), lambda i: (i, 0))]
    nl = xs[0].shape[0] // TM
    return [pl.BlockSpec((TM, width), lambda i: (jnp.minimum(i, nl - 1), 0)),
            pl.BlockSpec((TM, width), lambda i: (jnp.maximum(i - nl, 0), 0))]


def _token_tile(x_refs, lat_tiles):
    if len(x_refs) == 1:
        return x_refs[0][...]
    return jnp.where(pl.program_id(0) < lat_tiles, x_refs[0][...], x_refs[1][...])


def _inproj_kernel(*refs, n_x, lat_tiles):
    x_refs, (mod_ref, nw_ref, w_ref), out_refs = refs[:n_x], refs[n_x:n_x + 3], refs[n_x + 3:-1]
    kt_ref = refs[-1]
    x = _token_tile(x_refs, lat_tiles)
    h = _norm_mod(x, nw_ref[...], mod_ref[0:1, :], mod_ref[1:2, :]).astype(BF16)
    c0 = 0
    for o_ref in out_refs:
        n = o_ref.shape[-1]
        res = jnp.dot(h, w_ref[:, c0:c0 + n], preferred_element_type=F32)
        o_ref[...] = res.astype(o_ref.dtype)
        if c0 == P_MLA:
            kw = ML_HEADS * ML_DK
            kt_ref[...] = res[:, kw:2 * kw].T.astype(kt_ref.dtype)
        c0 += n


def _inproj(l, xs, mod, norm_w, w, seq, batch):
    nt = sum(x.shape[0] for x in xs)
    d = xs[0].shape[1]
    row = functools.partial(_mod_row, seq=seq, batch=batch)
    widths = (P_MLA, P_ML, P_GD, P_GATE)
    dtypes = (BF16, BF16, BF16, F32)
    return pl.pallas_call(
        functools.partial(_inproj_kernel, n_x=len(xs), lat_tiles=xs[0].shape[0] // TM),
        grid=(nt // TM,),
        in_specs=_token_specs(xs, d) + [
            pl.BlockSpec((None, None, 6, d), lambda i: (l, row(i), 0, 0)),
            pl.BlockSpec((None, 1, d), lambda i: (l, 0, 0)),
            pl.BlockSpec((None, d, P_ALL), lambda i: (l, 0, 0), pipeline_mode=pl.Buffered(1)),
        ],
        out_specs=[pl.BlockSpec((TM, n), lambda i: (i, 0)) for n in widths]
        + [pl.BlockSpec((ML_HEADS * ML_DK, TM), lambda i: (0, i))],
        out_shape=[jax.ShapeDtypeStruct((nt, n), dt) for n, dt in zip(widths, dtypes)]
        + [jax.ShapeDtypeStruct((ML_HEADS * ML_DK, nt), BF16)],
        compiler_params=_cparams("parallel"),
        name="inproj",
    )(*xs, mod, norm_w[:, None, :], w)


def _swap16(x):
    n = x.shape[-1]
    lane = lax.broadcasted_iota(jnp.int32, x.shape, 1)
    return jnp.where(lane % 32 < 16, pltpu.roll(x, n - 16, 1), pltpu.roll(x, 16, 1))


def _mla_qkv_kernel(p_ref, qn_ref, kvn_ref, wuq_ref, wukv_ref, cos_ref, sin_ref, q_ref, k_ref, v_ref):
    p = p_ref[...].astype(F32)
    cq = p[:, :Q_LORA]
    ckv = p[:, Q_LORA:Q_LORA + KV_LORA]
    kpe = p[:, Q_LORA + KV_LORA:]
    cos = cos_ref[...]
    sin = sin_ref[...]
    qn = (_rms(cq) * qn_ref[...]).astype(BF16)
    q = jnp.dot(qn, wuq_ref[...], preferred_element_type=F32) * (QK_DIM ** -0.5 * LOG2_E)
    nope_w = MLA_HEADS * QK_NOPE
    qpe = q[:, nope_w:]
    qpe = qpe * cos + _swap16(qpe) * sin
    kvn = (_rms(ckv) * kvn_ref[...]).astype(BF16)
    kv = jnp.dot(kvn, wukv_ref[...], preferred_element_type=F32)
    kpe = kpe * cos[:, :LANES] + _swap16(kpe) * sin[:, :LANES]
    kpe = kpe[:, :QK_ROPE]
    ones = jnp.ones((q.shape[0], LANES), BF16)
    for h in range(MLA_HEADS):
        qh = jnp.concatenate([q[:, h * QK_NOPE:(h + 1) * QK_NOPE], qpe[:, h * QK_ROPE:(h + 1) * QK_ROPE]], axis=-1)
        kh = jnp.concatenate([kv[:, h * QK_NOPE:(h + 1) * QK_NOPE], kpe], axis=-1)
        q_ref[h] = qh.astype(BF16)
        k_ref[h] = kh.astype(BF16)
        v_ref[h, :, :V_HEAD] = kv[:, nope_w + h * V_HEAD:nope_w + (h + 1) * V_HEAD].astype(BF16)
        v_ref[h, :, V_HEAD:] = ones


def _mla_qkv_call(l, p_mla, q_norm, kv_norm, wuq, wukv, cos_t, sin_t, seq, n_lat_rows):
    nt = p_mla.shape[0]
    lat_tiles = seq // TM
    nl_tiles = n_lat_rows // TM
    pe_w = MLA_HEADS * QK_ROPE
    v1_w = V_HEAD + LANES

    def rope_block(i):
        return (jnp.where(i < nl_tiles, i % lat_tiles, lat_tiles), 0)

    return pl.pallas_call(
        _mla_qkv_kernel,
        grid=(nt // TM,),
        in_specs=[
            pl.BlockSpec((TM, P_MLA), lambda i: (i, 0)),
            pl.BlockSpec((None, 1, Q_LORA), lambda i: (l, 0, 0)),
            pl.BlockSpec((None, 1, KV_LORA), lambda i: (l, 0, 0)),
            pl.BlockSpec((None,) + wuq.shape[1:], lambda i: (l, 0, 0)),
            pl.BlockSpec((None,) + wukv.shape[1:], lambda i: (l, 0, 0)),
            pl.BlockSpec((TM, pe_w), rope_block),
            pl.BlockSpec((TM, pe_w), rope_block),
        ],
        out_specs=[
            pl.BlockSpec((MLA_HEADS, TM, QK_DIM), lambda i: (0, i, 0)),
            pl.BlockSpec((MLA_HEADS, TM, QK_DIM), lambda i: (0, i, 0)),
            pl.BlockSpec((MLA_HEADS, TM, v1_w), lambda i: (0, i, 0)),
        ],
        out_shape=[
            jax.ShapeDtypeStruct((MLA_HEADS, nt, QK_DIM), BF16),
            jax.ShapeDtypeStruct((MLA_HEADS, nt, QK_DIM), BF16),
            jax.ShapeDtypeStruct((MLA_HEADS, nt, v1_w), BF16),
        ],
        compiler_params=_cparams("parallel"),
        name="mla_qkv",
    )(p_mla, q_norm[:, None, :], kv_norm[:, None, :], wuq, wukv, cos_t, sin_t)


KV_CHUNK = 512
ATT_TQ = 1024


def _attn_kernel(q_ref, *refs):
    o_ref = refs[-1]
    q = q_ref[...]
    chunks = []
    for k_ref, v_ref in zip(refs[0:-1:2], refs[1:-1:2]):
        n = k_ref.shape[0]
        step = min(KV_CHUNK, n)
        chunks += [(k_ref, v_ref, c0, step) for c0 in range(0, n, step)]
    scores = lambda c: lax.dot_general(q, c[0][c[2]:c[2] + c[3], :], NT_DIMS, preferred_element_type=F32)
    m = jnp.full((q.shape[0], 1), -jnp.inf, F32)
    acc = jnp.zeros((q.shape[0], V_HEAD + LANES), F32)
    s_next = scores(chunks[0])
    for idx, c in enumerate(chunks):
        s = s_next
        if idx + 1 < len(chunks):
            s_next = scores(chunks[idx + 1])
        m_new = jnp.maximum(m, jnp.max(s, axis=-1, keepdims=True))
        p = jnp.exp2(s - m_new).astype(BF16)
        acc = jnp.exp2(m - m_new) * acc + jnp.dot(p, c[1][c[2]:c[2] + c[3], :], preferred_element_type=F32)
        m = m_new
    o_ref[...] = (acc[:, :V_HEAD] / acc[:, V_HEAD:]).astype(o_ref.dtype)


def _attention(q, k, v1, batch, seq, nctx):
    tq = ATT_TQ
    nq = seq // tq
    ctx0 = batch * seq // nctx
    v1_w = v1.shape[-1]
    a_lat = pl.pallas_call(
        _attn_kernel,
        grid=(batch, MLA_HEADS, nq),
        in_specs=[
            pl.BlockSpec((None, tq, QK_DIM), lambda b, h, i: (h, b * nq + i, 0)),
            pl.BlockSpec((None, nctx, QK_DIM), lambda b, h, i: (h, ctx0 + b, 0)),
            pl.BlockSpec((None, nctx, v1_w), lambda b, h, i: (h, ctx0 + b, 0)),
            pl.BlockSpec((None, seq, QK_DIM), lambda b, h, i: (h, b, 0)),
            pl.BlockSpec((None, seq, v1_w), lambda b, h, i: (h, b, 0)),
        ],
        out_specs=pl.BlockSpec((tq, V_HEAD), lambda b, h, i: (b * nq + i, h)),
        out_shape=jax.ShapeDtypeStruct((batch * seq, MLA_W), BF16),
        compiler_params=_cparams("parallel", "parallel", "arbitrary"),
        name="attn_lat",
    )(q, k, v1, k, v1)
    a_ctx = pl.pallas_call(
        _attn_kernel,
        grid=(batch, MLA_HEADS),
        in_specs=[
            pl.BlockSpec((None, nctx, QK_DIM), lambda b, h: (h, ctx0 + b, 0)),
            pl.BlockSpec((None, nctx, QK_DIM), lambda b, h: (h, ctx0 + b, 0)),
            pl.BlockSpec((None, nctx, v1_w), lambda b, h: (h, ctx0 + b, 0)),
        ],
        out_specs=pl.BlockSpec((nctx, V_HEAD), lambda b, h: (b, h)),
        out_shape=jax.ShapeDtypeStruct((batch * nctx, MLA_W), BF16),
        compiler_params=_cparams("parallel", "parallel"),
        name="attn_ctx",
    )(q, k, v1)
    return a_lat, a_ctx


def _gates_kernel(g_ref, par_ref, col_ref, row_ref):
    tb = g_ref.shape[0]
    z = g_ref[...] + par_ref[0:1, :]
    neg_a = -jnp.exp(par_ref[1:2, :])
    lane = lax.broadcasted_iota(jnp.int32, z.shape, 1)
    kind = (lane // 4) % 4
    is_ml = lane < 16
    is_cum = (kind % 2 == 1) & (lane < 32)
    is_bwd = kind >= 2
    soft = jnp.log(1.0 + jnp.exp(-jnp.abs(z)))
    log_sig = jnp.minimum(z, 0.0) - soft
    softplus = jnp.maximum(z, 0.0) + soft
    pre = jnp.where(is_ml, jnp.where(is_cum, log_sig, z), jnp.where(is_cum, neg_a * softplus, _sigmoid(z)))
    r = lax.broadcasted_iota(jnp.int32, (tb, tb), 0)
    c = lax.broadcasted_iota(jnp.int32, (tb, tb), 1)
    same = (r // CHUNK) == (c // CHUNK)
    lower = jnp.where(same & (c <= r), 1.0, 0.0).astype(F32)
    upper = jnp.where(same & (c >= r), 1.0, 0.0).astype(F32)
    cum_f = jnp.dot(lower, pre, preferred_element_type=F32, precision=lax.Precision.HIGHEST)
    cum_b = jnp.dot(upper, pre, preferred_element_type=F32, precision=lax.Precision.HIGHEST)
    out = jnp.where(is_cum, jnp.where(is_bwd, cum_b, cum_f), pre)
    is_mli = is_ml & jnp.logical_not(is_cum)
    d = out - pltpu.roll(out, P_GATE - ML_HEADS, 1)
    t = lax.broadcasted_iota(jnp.int32, z.shape, 0) % CHUNK
    run = d
    shift = 1
    while shift < CHUNK:
        prev_f = jnp.where(t >= shift, pltpu.roll(run, shift, 0), -jnp.inf)
        prev_b = jnp.where(t < CHUNK - shift, pltpu.roll(run, tb - shift, 0), -jnp.inf)
        run = jnp.maximum(run, jnp.where(is_bwd, prev_b, prev_f))
        shift *= 2
    col_ref[...] = jnp.where(is_mli, run, out)
    row_ref[...] = jnp.where(is_mli, d, out).T


def _gates(l, g, par):
    nt = g.shape[0]
    return pl.pallas_call(
        _gates_kernel,
        grid=(nt // TB,),
        in_specs=[
            pl.BlockSpec((TB, P_GATE), lambda i: (i, 0)),
            pl.BlockSpec((None, 8, P_GATE), lambda i: (l, 0, 0)),
        ],
        out_specs=[
            pl.BlockSpec((TB, P_GATE), lambda i: (i, 0)),
            pl.BlockSpec((P_GATE, TB), lambda i: (0, i)),
        ],
        out_shape=[
            jax.ShapeDtypeStruct((nt, P_GATE), F32),
            jax.ShapeDtypeStruct((P_GATE, nt), F32),
        ],
        compiler_params=_cparams("parallel"),
        name="gates",
    )(g, par)


def _scan_blocks(batch, seq, nctx):
    ncb = nctx // TB
    nlb = seq // TB
    ctx0 = batch * nlb

    def fwd(b, j):
        return jnp.where(j < ncb, ctx0 + b * ncb + j, b * nlb + (j - ncb))

    def bwd(b, j):
        return jnp.where(j < ncb, ctx0 + b * ncb + (ncb - 1 - j), b * nlb + (nlb - 1 - (j - ncb)))

    return ncb + nlb, fwd, bwd


def _dot(a, b, dims=NN_DIMS):
    return lax.dot_general(a.astype(BF16), b.astype(BF16), dims, preferred_element_type=F32)


def _scan_units(nheads):
    return [(d, h) for d in range(2) for h in range(nheads)]


def _mlstm_kernel(xf_ref, xb_ref, ktf_ref, ktb_ref, gcf_ref, gcb_ref, grf_ref, grb_ref, hf_ref, hb_ref, c_ref, m_ref):
    @pl.when(pl.program_id(1) == 0)
    def _():
        c_ref[...] = jnp.zeros_like(c_ref)
        m_ref[...] = jnp.full(m_ref.shape, M_INIT, F32)

    nch = TB // CHUNK
    row = lax.broadcasted_iota(jnp.int32, (CHUNK, CHUNK), 0)
    col = lax.broadcasted_iota(jnp.int32, (CHUNK, CHUNK), 1)
    kscale = ML_DK ** -0.5
    qw = ML_HEADS * ML_DK
    refs = ((xf_ref, ktf_ref, gcf_ref, grf_ref, hf_ref), (xb_ref, ktb_ref, gcb_ref, grb_ref, hb_ref))
    units = _scan_units(ML_HEADS)

    ones = jnp.ones((CHUNK, LANES), BF16)
    sel_row = lax.broadcasted_iota(jnp.int32, (P_GATE, 2 * ML_HEADS * LANES), 0)
    sel_blk = lax.broadcasted_iota(jnp.int32, (P_GATE, 2 * ML_HEADS * LANES), 1) // LANES
    loc = {}
    for ci in range(nch):
        for d in range(2):
            x_ref, kt_ref, gc_ref, gr_ref, _ = refs[d]
            ch = ci if d == 0 else nch - 1 - ci
            rs = slice(ch * CHUNK, (ch + 1) * CHUNK)
            src_col = jnp.where(sel_blk < ML_HEADS, d * 8 + ML_HEADS + sel_blk, d * 8 + sel_blk - ML_HEADS)
            select = jnp.where(sel_row == src_col, 1.0, 0.0).astype(BF16)
            g = gc_ref[rs, :]
            g_hi = g.astype(BF16)
            g_lo = (g - g_hi.astype(F32)).astype(BF16)
            rep = (jnp.dot(g_hi, select, preferred_element_type=F32)
                   + jnp.dot(g_lo, select, preferred_element_type=F32))
            mask = (col <= row) if d == 0 else (col >= row)
            last = CHUNK - 1 if d == 0 else 0
            for h in range(ML_HEADS):
                b_rep = rep[:, h * LANES:(h + 1) * LANES]
                pm_rep = rep[:, (ML_HEADS + h) * LANES:(ML_HEADS + h + 1) * LANES]
                d_row = gr_ref[d * 8 + h:d * 8 + h + 1, rs]
                b_end = b_rep[last:last + 1, :]
                loc[ci, d, h] = dict(
                    rs=rs, b_rep=b_rep, b_end=b_end,
                    log_d=jnp.where(mask, b_rep[:, :CHUNK] + d_row, -jnp.inf),
                    log_w=b_end[:, :CHUNK] + d_row,
                    q=x_ref[rs, h * ML_DK:(h + 1) * ML_DK],
                    kt=kt_ref[h * ML_DK:(h + 1) * ML_DK, rs],
                    v1=jnp.concatenate([x_ref[rs, 2 * qw + h * ML_DV:2 * qw + (h + 1) * ML_DV], ones], axis=-1),
                    row_max=b_rep + pm_rep,
                    w_max=b_end + pm_rep[last:last + 1, :])
    for u in loc.values():
        u["qk"] = _dot(u["q"], u["kt"])

    state = {(d, h): (c_ref[d * ML_HEADS + h], m_ref[d * ML_HEADS + h]) for d, h in units}
    for ci in range(nch):
        cur = [(dh, loc[(ci,) + dh], state[dh]) for dh in units]
        for _, u, (cn, m) in cur:
            log_inter = u["b_rep"] + m
            m_t = jnp.maximum(log_inter, u["row_max"])
            u["w_inter"] = jnp.exp(log_inter - m_t)
            u["floor"] = jnp.exp(-m_t)
            u["s"] = u["qk"] * jnp.exp(u["log_d"] - m_t[:, :CHUNK]) * kscale
        for _, u, (cn, m) in cur:
            u["qc"] = _dot(u["q"], cn)
            u["sv"] = _dot(u["s"], u["v1"])
        for (d, h), u, (cn, m) in cur:
            num = u["w_inter"] * u["qc"][:, :ML_DV] + u["sv"][:, :ML_DV]
            den = u["w_inter"] * u["qc"][:, ML_DV:] + u["sv"][:, ML_DV:]
            refs[d][4][u["rs"], h * ML_DV:(h + 1) * ML_DV] = num / jnp.maximum(jnp.abs(den), u["floor"])
            m_new = jnp.maximum(u["b_end"] + m, u["w_max"])
            u["decay"] = jnp.exp(u["b_end"] + m - m_new)
            u["m_new"] = m_new
            u["kwt"] = u["kt"].astype(F32) * (jnp.exp(u["log_w"] - m_new[:, :CHUNK]) * kscale)
        for _, u, _st in cur:
            u["upd"] = _dot(u["kwt"], u["v1"])
        for dh, u, (cn, m) in cur:
            state[dh] = (jnp.concatenate([u["decay"], u["decay"]], axis=-1) * cn + u["upd"], u["m_new"])
    for d, h in units:
        st = d * ML_HEADS + h
        c_ref[st], m_ref[st] = state[(d, h)]


def _mlstm(p_ml, kt, gcol, grow, batch, seq, nctx):
    nt = p_ml.shape[0]
    nsteps, fwd, bwd = _scan_blocks(batch, seq, nctx)
    xw = 2 * ML_HEADS * ML_DK + ML_W
    nst = 2 * ML_HEADS
    return pl.pallas_call(
        _mlstm_kernel,
        grid=(batch, nsteps),
        in_specs=[
            pl.BlockSpec((TB, xw), lambda b, j: (fwd(b, j), 0)),
            pl.BlockSpec((TB, xw), lambda b, j: (bwd(b, j), 0)),
            pl.BlockSpec((ML_HEADS * ML_DK, TB), lambda b, j: (0, fwd(b, j))),
            pl.BlockSpec((ML_HEADS * ML_DK, TB), lambda b, j: (0, bwd(b, j))),
            pl.BlockSpec((TB, P_GATE), lambda b, j: (fwd(b, j), 0)),
            pl.BlockSpec((TB, P_GATE), lambda b, j: (bwd(b, j), 0)),
            pl.BlockSpec((P_GATE, TB), lambda b, j: (0, fwd(b, j))),
            pl.BlockSpec((P_GATE, TB), lambda b, j: (0, bwd(b, j))),
        ],
        out_specs=[
            pl.BlockSpec((TB, ML_W), lambda b, j: (fwd(b, j), 0)),
            pl.BlockSpec((TB, ML_W), lambda b, j: (bwd(b, j), 0)),
        ],
        out_shape=[jax.ShapeDtypeStruct((nt, ML_W), F32)] * 2,
        scratch_shapes=[
            pltpu.VMEM((nst, ML_DK, ML_DV + LANES), F32),
            pltpu.VMEM((nst, 1, LANES), F32),
        ],
        compiler_params=_cparams("parallel", "arbitrary"),
        name="mlstm",
    )(p_ml, p_ml, kt, kt, gcol, gcol, grow, grow)


def _gdn_prep_kernel(x_ref, prev_ref, next_ref, w_ref, o_ref, buf_ref, *, seq_blocks, lat_blocks):
    i = pl.program_id(0)
    in_lat = i < lat_blocks
    first = jnp.where(in_lat, i % seq_blocks == 0, True)
    last = jnp.where(in_lat, i % seq_blocks == seq_blocks - 1, True)
    tb = x_ref.shape[0]
    prev = prev_ref[...].astype(F32)[HALO - 8:, :]
    nxt = next_ref[...].astype(F32)[:8, :]
    buf_ref[0:8, :] = jnp.where(first, 0.0, prev)
    buf_ref[8:8 + tb, :] = x_ref[...].astype(F32)
    buf_ref[8 + tb:16 + tb, :] = jnp.where(last, 0.0, nxt)
    half = CONV_W // 2
    y = None
    for j in range(CONV_W):
        term = buf_ref[8 - half + j:8 - half + j + tb, :] * w_ref[j:j + 1, :]
        y = term if y is None else y + term
    y = y * _sigmoid(y)
    nk = GD_HEADS * GD_DK
    for h in range(2 * GD_HEADS):
        cols = slice(h * GD_DK, (h + 1) * GD_DK)
        seg = y[:, cols]
        seg = seg * lax.rsqrt(jnp.sum(seg * seg, axis=-1, keepdims=True) + EPS)
        if h < GD_HEADS:
            seg = seg * (GD_DK ** -0.5)
        o_ref[:, cols] = seg
    o_ref[:, 2 * nk:] = y[:, 2 * nk:]


def _gdn_prep(l, p_gd, conv_w, batch, seq, nctx):
    nt = p_gd.shape[0]
    cw = 2 * GD_HEADS * GD_DK + GD_W
    tb = nctx
    hb = tb // HALO
    nblk = nt // tb
    kern = functools.partial(_gdn_prep_kernel, seq_blocks=seq // tb, lat_blocks=batch * seq // tb)
    return pl.pallas_call(
        kern,
        grid=(nblk,),
        in_specs=[
            pl.BlockSpec((tb, cw), lambda i: (i, 0)),
            pl.BlockSpec((HALO, cw), lambda i: (jnp.maximum(i * hb - 1, 0), 0)),
            pl.BlockSpec((HALO, cw), lambda i: (jnp.minimum((i + 1) * hb, nblk * hb - 1), 0)),
            pl.BlockSpec((None, CONV_W, cw), lambda i: (l, 0, 0)),
        ],
        out_specs=pl.BlockSpec((tb, cw), lambda i: (i, 0)),
        out_shape=jax.ShapeDtypeStruct((nt, cw), F32),
        scratch_shapes=[pltpu.VMEM((tb + 16, cw), F32)],
        compiler_params=_cparams("parallel"),
        name="gdn_prep",
    )(p_gd, p_gd, p_gd, conv_w)


INV_BLOCK = 16


def _unit_triangular_inverses(mats):
    n = mats[0].shape[0]
    row = lax.broadcasted_iota(jnp.int32, (n, n), 0)
    col = lax.broadcasted_iota(jnp.int32, (n, n), 1)
    eye = jnp.where(row == col, 1.0, 0.0)
    same_block = row // INV_BLOCK == col // INV_BLOCK
    ps = [jnp.where(same_block, a, 0.0) for a in mats]
    xs = [eye - p for p in ps]
    for _ in range((INV_BLOCK - 1).bit_length() - 1):
        ps = [_dot(p, p) for p in ps]
        xs = [x + xp for x, xp in zip(xs, [_dot(x, p) for x, p in zip(xs, ps)])]
    size = INV_BLOCK
    while size < n:
        sel = (row // (2 * size) == col // (2 * size)) & (row // size != col // size)
        ts = [_dot(x, jnp.where(sel, a, 0.0)) for x, a in zip(xs, mats)]
        xs = [x - tx for x, tx in zip(xs, [_dot(t, x) for t, x in zip(ts, xs)])]
        size *= 2
    return xs


def _gdn_kernel(xf_ref, xb_ref, gcf_ref, gcb_ref, grf_ref, grb_ref, of_ref, ob_ref, s_ref):
    @pl.when(pl.program_id(1) == 0)
    def _():
        s_ref[...] = jnp.zeros_like(s_ref)

    nch = TB // CHUNK
    row = lax.broadcasted_iota(jnp.int32, (CHUNK, CHUNK), 0)
    col = lax.broadcasted_iota(jnp.int32, (CHUNK, CHUNK), 1)
    nk = GD_HEADS * GD_DK
    refs = ((xf_ref, gcf_ref, grf_ref, of_ref), (xb_ref, gcb_ref, grb_ref, ob_ref))
    units = _scan_units(GD_HEADS)

    loc = {}
    for ci in range(nch):
        for d, h in units:
            x_ref, gc_ref, gr_ref, _ = refs[d]
            ch = ci if d == 0 else nch - 1 - ci
            rs = slice(ch * CHUNK, (ch + 1) * CHUNK)
            cb = 16 + d * 8 + h
            cg = 16 + d * 8 + 4 + h
            beta = jnp.broadcast_to(gc_ref[rs, cb:cb + 1], (CHUNK, LANES))
            g_col = jnp.broadcast_to(gc_ref[rs, cg:cg + 1], (CHUNK, LANES))
            g_row = gr_ref[cg:cg + 1, rs]
            g_end = g_col[CHUNK - 1:CHUNK, :] if d == 0 else g_col[0:1, :]
            incl = (col <= row) if d == 0 else (col >= row)
            q = x_ref[rs, h * GD_DK:(h + 1) * GD_DK]
            k = x_ref[rs, nk + h * GD_DK:nk + (h + 1) * GD_DK]
            v = x_ref[rs, 2 * nk + h * GD_DV:2 * nk + (h + 1) * GD_DV]
            kb = k * beta
            e_col = jnp.exp(g_col)
            loc[ci, d, h] = dict(
                rs=rs, k=k, kb=kb, q=q,
                strict=(col < row) if d == 0 else (col > row),
                decay=jnp.exp(jnp.where(incl, g_col[:, :CHUNK] - g_row, -jnp.inf)),
                rhs=jnp.concatenate([v * beta, kb * e_col], axis=-1),
                qg=q * e_col, kd=k * jnp.exp(g_end - g_col), e_end=jnp.exp(g_end))
    us = list(loc.values())
    for u in us:
        u["kk"] = _dot(u["kb"], u["k"], NT_DIMS)
        u["attn"] = _dot(u["q"], u["k"], NT_DIMS) * u["decay"]
    tinvs = _unit_triangular_inverses([jnp.where(u["strict"], u["kk"] * u["decay"], 0.0) for u in us])
    for u, tinv in zip(us, tinvs):
        u["uw"] = _dot(tinv, u["rhs"])

    state = {(d, h): s_ref[d * GD_HEADS + h] for d, h in units}
    for ci in range(nch):
        cur = [(dh, loc[(ci,) + dh]) for dh in units]
        for dh, u in cur:
            u["ws"] = _dot(u["uw"][:, GD_DV:], state[dh])
            u["qs"] = _dot(u["qg"], state[dh])
        for dh, u in cur:
            u["v_new"] = u["uw"][:, :GD_DV] - u["ws"]
        for dh, u in cur:
            u["av"] = _dot(u["attn"], u["v_new"])
            u["upd"] = _dot(u["kd"], u["v_new"], TN_DIMS)
        for (d, h), u in cur:
            refs[d][3][u["rs"], h * GD_DV:(h + 1) * GD_DV] = u["qs"] + u["av"]
            state[(d, h)] = state[(d, h)] * u["e_end"] + u["upd"]
    for d, h in units:
        s_ref[d * GD_HEADS + h] = state[(d, h)]


def _gdn(qkv, gcol, grow, batch, seq, nctx):
    nt = qkv.shape[0]
    nsteps, fwd, bwd = _scan_blocks(batch, seq, nctx)
    cw = qkv.shape[1]
    return pl.pallas_call(
        _gdn_kernel,
        grid=(batch, nsteps),
        in_specs=[
            pl.BlockSpec((TB, cw), lambda b, j: (fwd(b, j), 0)),
            pl.BlockSpec((TB, cw), lambda b, j: (bwd(b, j), 0)),
            pl.BlockSpec((TB, P_GATE), lambda b, j: (fwd(b, j), 0)),
            pl.BlockSpec((TB, P_GATE), lambda b, j: (bwd(b, j), 0)),
            pl.BlockSpec((P_GATE, TB), lambda b, j: (0, fwd(b, j))),
            pl.BlockSpec((P_GATE, TB), lambda b, j: (0, bwd(b, j))),
        ],
        out_specs=[
            pl.BlockSpec((TB, GD_W), lambda b, j: (fwd(b, j), 0)),
            pl.BlockSpec((TB, GD_W), lambda b, j: (bwd(b, j), 0)),
        ],
        out_shape=[jax.ShapeDtypeStruct((nt, GD_W), F32)] * 2,
        scratch_shapes=[pltpu.VMEM((2 * GD_HEADS, GD_DK, GD_DV), F32)],
        compiler_params=_cparams("parallel", "arbitrary"),
        name="gdn",
    )(qkv, qkv, gcol, gcol, grow, grow)


def _head_rms(y, gain, nheads, width):
    parts = []
    for h in range(nheads):
        seg = y[:, h * width:(h + 1) * width]
        parts.append(_rms(seg))
    return jnp.concatenate(parts, axis=-1) * gain


def _merge_kernel(*refs, n_x, lat_tiles):
    x_refs = refs[:n_x]
    (mod_ref, al_ref, ac_ref, hf_ref, hb_ref, o_ref, gf_ref, gb_ref, z_ref,
     an_ref, mn_ref, gn_ref, w_ref, out_ref) = refs[n_x:]
    a = _token_tile((al_ref, ac_ref), lat_tiles).astype(F32)
    ya = _head_rms(a, an_ref[...], MLA_HEADS, V_HEAD)
    ym = _head_rms(hf_ref[...] + hb_ref[...], mn_ref[...], ML_HEADS, ML_DV) * _sigmoid(o_ref[...].astype(F32))
    z = z_ref[...].astype(F32)
    yg = _head_rms(gf_ref[...] + gb_ref[...], gn_ref[...], GD_HEADS, GD_DV) * (z * _sigmoid(z))
    y = jnp.dot(ya.astype(BF16), w_ref[0:MLA_W, :], preferred_element_type=F32)
    y = y + jnp.dot(ym.astype(BF16), w_ref[MLA_W:MLA_W + ML_W, :], preferred_element_type=F32)
    y = y + jnp.dot(yg.astype(BF16), w_ref[MLA_W + ML_W:, :], preferred_element_type=F32)
    out_ref[...] = _token_tile(x_refs, lat_tiles) + mod_ref[2:3, :] * y


def _merge(l, xs, mod, a_lat, a_ctx, hf, hb, p_ml, gf, gb, p_gd, mla_norm, ml_norm, gd_norm, w_out, seq, batch, nrows):
    d = xs[0].shape[1]
    lat_tiles = a_lat.shape[0] // TM
    assert a_ctx.shape[0] == TM
    row = functools.partial(_mod_row, seq=seq, batch=batch)
    tile = lambda w, cb=0: pl.BlockSpec((TM, w), lambda i: (i, cb))
    vec = lambda w: pl.BlockSpec((None, 1, w), lambda i: (l, 0, 0))
    return pl.pallas_call(
        functools.partial(_merge_kernel, n_x=len(xs), lat_tiles=lat_tiles),
        grid=(nrows // TM,),
        in_specs=_token_specs(xs, d) + [
            pl.BlockSpec((None, None, 6, d), lambda i: (l, row(i), 0, 0)),
            *_token_specs((a_lat, a_ctx), MLA_W),
            tile(ML_W), tile(ML_W), tile(ML_W, 2),
            tile(GD_W), tile(GD_W), tile(GD_W, 3),
            vec(MLA_W), vec(ML_W), vec(GD_W),
            pl.BlockSpec((None, MIX_W, d), lambda i: (l, 0, 0)),
        ],
        out_specs=tile(d),
        out_shape=jax.ShapeDtypeStruct((nrows, d), F32),
        compiler_params=_cparams("parallel"),
        name="merge",
    )(*xs, mod, a_lat, a_ctx, hf, hb, p_ml, gf, gb, p_gd, mla_norm[:, None, :], ml_norm[:, None, :],
      jnp.tile(gd_norm, (1, GD_HEADS))[:, None, :], w_out)


def _mlp_kernel(x_ref, mod_ref, nw_ref, w1_ref, w2_ref, out_ref, h_ref, acc_ref):
    j = pl.program_id(1)

    @pl.when(j == 0)
    def _():
        h_ref[...] = _norm_mod(x_ref[...], nw_ref[...], mod_ref[3:4, :], mod_ref[4:5, :]).astype(BF16)
        acc_ref[...] = jnp.zeros_like(acc_ref)

    a = jnp.maximum(jnp.dot(h_ref[...], w1_ref[...], preferred_element_type=F32), 0.0)
    acc_ref[...] += jnp.dot((a * a).astype(BF16), w2_ref[...], preferred_element_type=F32)

    @pl.when(j == pl.num_programs(1) - 1)
    def _():
        out_ref[...] = x_ref[...] + mod_ref[5:6, :] * acc_ref[...]


def _mlp(l, xg, mod, norm_w, w1, w2, seq, batch, nrows):
    d = xg.shape[1]
    f = w1.shape[-1]
    tf = min(MLP_TF, f)
    row = functools.partial(_mod_row, seq=seq, batch=batch)
    return pl.pallas_call(
        _mlp_kernel,
        grid=(nrows // TM, f // tf),
        in_specs=[
            pl.BlockSpec((TM, d), lambda i, j: (i, 0)),
            pl.BlockSpec((None, None, 6, d), lambda i, j: (l, row(i), 0, 0)),
            pl.BlockSpec((None, 1, d), lambda i, j: (l, 0, 0)),
            pl.BlockSpec((None, d, tf), lambda i, j: (l, 0, j)),
            pl.BlockSpec((None, tf, d), lambda i, j: (l, j, 0)),
        ],
        out_specs=pl.BlockSpec((TM, d), lambda i, j: (i, 0)),
        out_shape=jax.ShapeDtypeStruct((nrows, d), F32),
        scratch_shapes=[pltpu.VMEM((TM, d), BF16), pltpu.VMEM((TM, d), F32)],
        compiler_params=_cparams("parallel", "arbitrary"),
        name="mlp",
    )(xg, mod, norm_w[:, None, :], w1, w2)


def _final_kernel(x_ref, w_ref, o_ref):
    o_ref[...] = _rms(x_ref[...]) * w_ref[...]


def _final_norm(xg, w, nrows):
    d = xg.shape[1]
    return pl.pallas_call(
        _final_kernel,
        grid=(nrows // TM,),
        in_specs=[pl.BlockSpec((TM, d), lambda i: (i, 0)), pl.BlockSpec((1, d), lambda i: (0, 0))],
        out_specs=pl.BlockSpec((TM, d), lambda i: (i, 0)),
        out_shape=jax.ShapeDtypeStruct((nrows, d), F32),
        compiler_params=_cparams("parallel"),
        name="final_norm",
    )(xg, w.reshape(1, d))


def _regroup_w_in(w_in):
    nl, d, _ = w_in.shape
    o_ml = Q_LORA + KV_LORA + QK_ROPE
    o_mlg = o_ml + P_ML
    o_gd = o_mlg + 4 * ML_HEADS
    o_gdg = o_gd + P_GD
    zeros = lambda n: jnp.zeros((nl, d, n), w_in.dtype)
    return jnp.concatenate([
        w_in[..., :o_ml], zeros(P_MLA - o_ml),
        w_in[..., o_ml:o_mlg],
        w_in[..., o_gd:o_gdg],
        w_in[..., o_mlg:o_gd], w_in[..., o_gdg:], zeros(P_GATE - 4 * ML_HEADS - 4 * GD_HEADS),
    ], axis=-1).astype(BF16)


def _regroup_heads(w, first):
    nl, kdim, _ = w.shape
    w4 = w.reshape(nl, kdim, MLA_HEADS, -1)
    return jnp.concatenate([w4[..., :first].reshape(nl, kdim, -1), w4[..., first:].reshape(nl, kdim, -1)],
                           axis=-1).astype(BF16)


def _rope_tables(seq):
    half = QK_ROPE // 2
    t = jnp.arange(seq, dtype=jnp.int32)
    inv = ROPE_BASE ** (-jnp.arange(0, half, 2, dtype=F32) / half)
    ang_r = (t // GRID_W).astype(F32)[:, None] * inv
    ang_c = (t % GRID_W).astype(F32)[:, None] * inv
    cos = jnp.concatenate([jnp.cos(ang_r)] * 2 + [jnp.cos(ang_c)] * 2, axis=-1)
    sin = jnp.concatenate([-jnp.sin(ang_r), jnp.sin(ang_r), -jnp.sin(ang_c), jnp.sin(ang_c)], axis=-1)
    cos = jnp.concatenate([cos, jnp.ones((TM, QK_ROPE), F32)], axis=0)
    sin = jnp.concatenate([sin, jnp.zeros((TM, QK_ROPE), F32)], axis=0)
    return jnp.tile(cos, (1, MLA_HEADS)), jnp.tile(sin, (1, MLA_HEADS))


def _gate_params(ml_gate_bias, gd_a_log, gd_dt_bias):
    nl = ml_gate_bias.shape[0]
    zh = jnp.zeros((nl, GD_HEADS), F32)
    bias = jnp.concatenate([ml_gate_bias, zh, gd_dt_bias[:, 0], zh, gd_dt_bias[:, 1]], axis=-1)
    alog = jnp.concatenate([jnp.zeros((nl, 4 * ML_HEADS), F32), zh, gd_a_log[:, 0], zh, gd_a_log[:, 1]], axis=-1)
    par = jnp.stack([bias, alog], axis=1)
    return jnp.pad(par, ((0, 0), (0, 6), (0, P_GATE - par.shape[-1])))


def kernel(x, c, ctx, c_ctx, w_ada, b_ada, norm1, norm2, w_in, mla_q_norm, mla_w_uq, mla_kv_norm, mla_w_ukv, mla_out_norm, ml_gate_bias, ml_out_norm, gd_conv, gd_a_log, gd_dt_bias, gd_out_norm, w_out, w_mlp1, w_mlp2, final_norm):
    batch, seq, d = x.shape
    nctx = ctx.shape[1]
    depth = w_ada.shape[0]
    n_lat = batch * seq
    n_all = n_lat + batch * nctx
    assert seq % TM == 0 and (batch * nctx) % TM == 0 and nctx % TB == 0 and TM % nctx == 0

    xs = (x.reshape(n_lat, d), ctx.reshape(batch * nctx, d))
    mod_all = _ada(jnp.concatenate([c, c_ctx[None]], axis=0), w_ada, b_ada).reshape(depth, batch + 1, 6, d)
    w_in_g = _regroup_w_in(w_in)
    wuq_g = _regroup_heads(mla_w_uq, QK_NOPE)
    wukv_g = _regroup_heads(mla_w_ukv, QK_NOPE)
    w_out_b = w_out.astype(BF16)
    w1_b = w_mlp1.astype(BF16)
    w2_b = w_mlp2.astype(BF16)
    cos_t, sin_t = _rope_tables(seq)
    gate_par = _gate_params(ml_gate_bias, gd_a_log, gd_dt_bias)

    for l in range(depth):
        last = l == depth - 1
        p_mla, p_ml, p_gd, p_gate, ml_kt = _inproj(l, xs, mod_all, norm1, w_in_g, seq, batch)
        q, k, v1 = _mla_qkv_call(l, p_mla, mla_q_norm, mla_kv_norm, wuq_g, wukv_g, cos_t, sin_t, seq, n_lat)
        a_lat, a_ctx = _attention(q, k, v1, batch, seq, nctx)
        gcol, grow = _gates(l, p_gate, gate_par)
        hf, hb = _mlstm(p_ml, ml_kt, gcol, grow, batch, seq, nctx)
        gd_qkv = _gdn_prep(l, p_gd, gd_conv, batch, seq, nctx)
        gf, gb = _gdn(gd_qkv, gcol, grow, batch, seq, nctx)
        nrows = n_lat if last else n_all
        xg = _merge(l, xs, mod_all, a_lat, a_ctx, hf, hb, p_ml, gf, gb, p_gd, mla_out_norm, ml_out_norm,
                    gd_out_norm, w_out_b, seq, batch, nrows)
        xs = (_mlp(l, xg, mod_all, norm2, w1_b, w2_b, seq, batch, nrows),)
    return _final_norm(xs[0], final_norm, n_lat).reshape(batch, seq, d)
```

```python
import functools

import jax
import jax.numpy as jnp
from jax import lax
from jax.experimental import pallas as pl
from jax.experimental.pallas import tpu as pltpu

F32 = jnp.float32
BF16 = jnp.bfloat16

EPS = 1e-6
LOG2_E = 1.4426950408889634
M_INIT = -1e30
GRID_W = 64
ROPE_BASE = 10000.0

MLA_HEADS = 8
Q_LORA = 512
KV_LORA = 256
QK_NOPE = 128
QK_ROPE = 64
V_HEAD = 128
QK_DIM = QK_NOPE + QK_ROPE
ML_HEADS = 4
ML_DK = 64
ML_DV = 128
GD_HEADS = 4
GD_DK = 128
GD_DV = 128
CHUNK = 64
CONV_W = 5

MLA_W = MLA_HEADS * V_HEAD
ML_W = ML_HEADS * ML_DV
GD_W = GD_HEADS * GD_DV
MIX_W = MLA_W + ML_W + GD_W

P_MLA = 896
P_ML = 1536
P_GD = 2048
P_GATE = 128
P_ALL = P_MLA + P_ML + P_GD + P_GATE

TM = 512
MLP_TF = 1024
ADA_COLS = 1024
TB = 256
HALO = 16
LANES = 128

VMEM_LIMIT_V7X = 56 * 1024 * 1024

NT_DIMS = (((1,), (1,)), ((), ()))
TN_DIMS = (((0,), (0,)), ((), ()))
NN_DIMS = (((1,), (0,)), ((), ()))


def _cparams(*sem):
    return pltpu.CompilerParams(dimension_semantics=sem, vmem_limit_bytes=VMEM_LIMIT_V7X)


def _sigmoid(x):
    return 1.0 / (1.0 + jnp.exp(-x))


def _rms(x):
    return x * lax.rsqrt(jnp.mean(x * x, axis=-1, keepdims=True) + EPS)


def _norm_mod(x, norm_w, shift, scale):
    return _rms(x) * norm_w * (1.0 + scale) + shift


def _mod_row(i, seq, batch):
    return jnp.minimum((i * TM) // seq, batch)


def _ada_kernel(cb_ref, w_ref, b_ref, o_ref, s_ref):
    @pl.when((pl.program_id(0) == 0) & (pl.program_id(1) == 0))
    def _():
        cb = cb_ref[...]
        s_ref[...] = cb * _sigmoid(cb)

    rows, d = cb_ref.shape[0], cb_ref.shape[1]
    tn = w_ref.shape[-1]
    tc = next(t for t in (ADA_COLS, ADA_COLS // 2, LANES) if tn % t == 0)
    sub = 8
    for c0 in range(0, tn, tc):
        def body(kb, accs):
            k0 = pl.multiple_of(kb * sub, sub)
            w = w_ref[pl.ds(k0, sub), c0:c0 + tc]
            return tuple(acc + w * jnp.tile(s_ref[r, pl.ds(k0, sub), :], (1, tc // LANES))
                         for r, acc in enumerate(accs))

        accs = lax.fori_loop(0, d // sub, body, tuple(jnp.zeros((sub, tc), F32) for _ in range(rows)), unroll=4)
        for r in range(rows):
            o_ref[r:r + 1, c0:c0 + tc] = jnp.sum(accs[r], axis=0, keepdims=True) + b_ref[:, c0:c0 + tc]


def _ada(cvec, w_ada, b_ada):
    nl, d, n6 = w_ada.shape
    r = cvec.shape[0]
    tn = next(t for t in (2048, 1536, 1024, 512, LANES) if n6 % t == 0)
    cb = jnp.broadcast_to(cvec[:, :, None], (r, d, LANES))
    return pl.pallas_call(
        _ada_kernel,
        grid=(nl, n6 // tn),
        in_specs=[
            pl.BlockSpec((r, d, LANES), lambda l, j: (0, 0, 0)),
            pl.BlockSpec((None, d, tn), lambda l, j: (l, 0, j)),
            pl.BlockSpec((None, 1, tn), lambda l, j: (l, 0, j)),
        ],
        out_specs=pl.BlockSpec((None, r, tn), lambda l, j: (l, 0, j)),
        out_shape=jax.ShapeDtypeStruct((nl, r, n6), F32),
        scratch_shapes=[pltpu.VMEM((r, d, LANES), F32)],
        compiler_params=_cparams("arbitrary", "arbitrary"),
        name="ada",
    )(cb, w_ada, b_ada.reshape(nl, 1, n6))


def _token_specs(xs, width):
    if len(xs) == 1:
        return [pl.BlockSpec((TM, width), lambda i: (i, 0))]
    nl = xs[0].shape[0] // TM
    return [pl.BlockSpec((TM, width), lambda i: (jnp.minimum(i, nl - 1), 0)),
            pl.BlockSpec((TM, width), lambda i: (jnp.maximum(i - nl, 0), 0))]


def _token_tile(x_refs, lat_tiles):
    if len(x_refs) == 1:
        return x_refs[0][...]
    return jnp.where(pl.program_id(0) < lat_tiles, x_refs[0][...], x_refs[1][...])


def _inproj_kernel(*refs, n_x, lat_tiles):
    x_refs, (mod_ref, nw_ref, w_ref), out_refs = refs[:n_x], refs[n_x:n_x + 3], refs[n_x + 3:-1]
    kt_ref = refs[-1]
    x = _token_tile(x_refs, lat_tiles)
    h = _norm_mod(x, nw_ref[...], mod_ref[0:1, :], mod_ref[1:2, :]).astype(BF16)
    c0 = 0
    for o_ref in out_refs:
        n = o_ref.shape[-1]
        res = jnp.dot(h, w_ref[:, c0:c0 + n], preferred_element_type=F32)
        o_ref[...] = res.astype(o_ref.dtype)
        if c0 == P_MLA:
            kw = ML_HEADS * ML_DK
            kt_ref[...] = res[:, kw:2 * kw].T.astype(kt_ref.dtype)
        c0 += n


def _inproj(l, xs, mod, norm_w, w, seq, batch):
    nt = sum(x.shape[0] for x in xs)
    d = xs[0].shape[1]
    row = functools.partial(_mod_row, seq=seq, batch=batch)
    widths = (P_MLA, P_ML, P_GD, P_GATE)
    dtypes = (BF16, BF16, BF16, F32)
    return pl.pallas_call(
        functools.partial(_inproj_kernel, n_x=len(xs), lat_tiles=xs[0].shape[0] // TM),
        grid=(nt // TM,),
        in_specs=_token_specs(xs, d) + [
            pl.BlockSpec((None, None, 6, d), lambda i: (l, row(i), 0, 0)),
            pl.BlockSpec((None, 1, d), lambda i: (l, 0, 0)),
            pl.BlockSpec((None, d, P_ALL), lambda i: (l, 0, 0), pipeline_mode=pl.Buffered(1)),
        ],
        out_specs=[pl.BlockSpec((TM, n), lambda i: (i, 0)) for n in widths]
        + [pl.BlockSpec((ML_HEADS * ML_DK, TM), lambda i: (0, i))],
        out_shape=[jax.ShapeDtypeStruct((nt, n), dt) for n, dt in zip(widths, dtypes)]
        + [jax.ShapeDtypeStruct((ML_HEADS * ML_DK, nt), BF16)],
        compiler_params=_cparams("parallel"),
        name="inproj",
    )(*xs, mod, norm_w[:, None, :], w)


def _swap16(x):
    n = x.shape[-1]
    lane = lax.broadcasted_iota(jnp.int32, x.shape, 1)
    return jnp.where(lane % 32 < 16, pltpu.roll(x, n - 16, 1), pltpu.roll(x, 16, 1))


def _mla_qkv_kernel(p_ref, qn_ref, kvn_ref, wuq_ref, wukv_ref, cos_ref, sin_ref, q_ref, k_ref, v_ref):
    p = p_ref[...].astype(F32)
    cq = p[:, :Q_LORA]
    ckv = p[:, Q_LORA:Q_LORA + KV_LORA]
    kpe = p[:, Q_LORA + KV_LORA:]
    cos = cos_ref[...]
    sin = sin_ref[...]
    qn = (_rms(cq) * qn_ref[...]).astype(BF16)
    q = jnp.dot(qn, wuq_ref[...], preferred_element_type=F32) * (QK_DIM ** -0.5 * LOG2_E)
    nope_w = MLA_HEADS * QK_NOPE
    qpe = q[:, nope_w:]
    qpe = qpe * cos + _swap16(qpe) * sin
    kvn = (_rms(ckv) * kvn_ref[...]).astype(BF16)
    kv = jnp.dot(kvn, wukv_ref[...], preferred_element_type=F32)
    kpe = kpe * cos[:, :LANES] + _swap16(kpe) * sin[:, :LANES]
    kpe = kpe[:, :QK_ROPE]
    ones = jnp.ones((q.shape[0], LANES), BF16)
    for h in range(MLA_HEADS):
        qh = jnp.concatenate([q[:, h * QK_NOPE:(h + 1) * QK_NOPE], qpe[:, h * QK_ROPE:(h + 1) * QK_ROPE]], axis=-1)
        kh = jnp.concatenate([kv[:, h * QK_NOPE:(h + 1) * QK_NOPE], kpe], axis=-1)
        q_ref[h] = qh.astype(BF16)
        k_ref[h] = kh.astype(BF16)
        v_ref[h, :, :V_HEAD] = kv[:, nope_w + h * V_HEAD:nope_w + (h + 1) * V_HEAD].astype(BF16)
        v_ref[h, :, V_HEAD:] = ones


def _mla_qkv_call(l, p_mla, q_norm, kv_norm, wuq, wukv, cos_t, sin_t, seq, n_lat_rows):
    nt = p_mla.shape[0]
    lat_tiles = seq // TM
    nl_tiles = n_lat_rows // TM
    pe_w = MLA_HEADS * QK_ROPE
    v1_w = V_HEAD + LANES

    def rope_block(i):
        return (jnp.where(i < nl_tiles, i % lat_tiles, lat_tiles), 0)

    return pl.pallas_call(
        _mla_qkv_kernel,
        grid=(nt // TM,),
        in_specs=[
            pl.BlockSpec((TM, P_MLA), lambda i: (i, 0)),
            pl.BlockSpec((None, 1, Q_LORA), lambda i: (l, 0, 0)),
            pl.BlockSpec((None, 1, KV_LORA), lambda i: (l, 0, 0)),
            pl.BlockSpec((None,) + wuq.shape[1:], lambda i: (l, 0, 0)),
            pl.BlockSpec((None,) + wukv.shape[1:], lambda i: (l, 0, 0)),
            pl.BlockSpec((TM, pe_w), rope_block),
            pl.BlockSpec((TM, pe_w), rope_block),
        ],
        out_specs=[
            pl.BlockSpec((MLA_HEADS, TM, QK_DIM), lambda i: (0, i, 0)),
            pl.BlockSpec((MLA_HEADS, TM, QK_DIM), lambda i: (0, i, 0)),
            pl.BlockSpec((MLA_HEADS, TM, v1_w), lambda i: (0, i, 0)),
        ],
        out_shape=[
            jax.ShapeDtypeStruct((MLA_HEADS, nt, QK_DIM), BF16),
            jax.ShapeDtypeStruct((MLA_HEADS, nt, QK_DIM), BF16),
            jax.ShapeDtypeStruct((MLA_HEADS, nt, v1_w), BF16),
        ],
        compiler_params=_cparams("parallel"),
        name="mla_qkv",
    )(p_mla, q_norm[:, None, :], kv_norm[:, None, :], wuq, wukv, cos_t, sin_t)


KV_CHUNK = 512
ATT_TQ = 1024


def _attn_kernel(q_ref, *refs):
    o_ref = refs[-1]
    q = q_ref[...]
    chunks = []
    for k_ref, v_ref in zip(refs[0:-1:2], refs[1:-1:2]):
        n = k_ref.shape[0]
        step = min(KV_CHUNK, n)
        chunks += [(k_ref, v_ref, c0, step) for c0 in range(0, n, step)]
    scores = lambda c: lax.dot_general(q, c[0][c[2]:c[2] + c[3], :], NT_DIMS, preferred_element_type=F32)
    m = jnp.full((q.shape[0], 1), -jnp.inf, F32)
    acc = jnp.zeros((q.shape[0], V_HEAD + LANES), F32)
    s_next = scores(chunks[0])
    for idx, c in enumerate(chunks):
        s = s_next
        if idx + 1 < len(chunks):
            s_next = scores(chunks[idx + 1])
        m_new = jnp.maximum(m, jnp.max(s, axis=-1, keepdims=True))
        p = jnp.exp2(s - m_new).astype(BF16)
        acc = jnp.exp2(m - m_new) * acc + jnp.dot(p, c[1][c[2]:c[2] + c[3], :], preferred_element_type=F32)
        m = m_new
    o_ref[...] = (acc[:, :V_HEAD] / acc[:, V_HEAD:]).astype(o_ref.dtype)


def _attention(q, k, v1, batch, seq, nctx):
    tq = ATT_TQ
    nq = seq // tq
    ctx0 = batch * seq // nctx
    v1_w = v1.shape[-1]
    a_lat = pl.pallas_call(
        _attn_kernel,
        grid=(batch, MLA_HEADS, nq),
        in_specs=[
            pl.BlockSpec((None, tq, QK_DIM), lambda b, h, i: (h, b * nq + i, 0)),
            pl.BlockSpec((None, nctx, QK_DIM), lambda b, h, i: (h, ctx0 + b, 0)),
            pl.BlockSpec((None, nctx, v1_w), lambda b, h, i: (h, ctx0 + b, 0)),
            pl.BlockSpec((None, seq, QK_DIM), lambda b, h, i: (h, b, 0)),
            pl.BlockSpec((None, seq, v1_w), lambda b, h, i: (h, b, 0)),
        ],
        out_specs=pl.BlockSpec((tq, V_HEAD), lambda b, h, i: (b * nq + i, h)),
        out_shape=jax.ShapeDtypeStruct((batch * seq, MLA_W), BF16),
        compiler_params=_cparams("parallel", "parallel", "arbitrary"),
        name="attn_lat",
    )(q, k, v1, k, v1)
    a_ctx = pl.pallas_call(
        _attn_kernel,
        grid=(batch, MLA_HEADS),
        in_specs=[
            pl.BlockSpec((None, nctx, QK_DIM), lambda b, h: (h, ctx0 + b, 0)),
            pl.BlockSpec((None, nctx, QK_DIM), lambda b, h: (h, ctx0 + b, 0)),
            pl.BlockSpec((None, nctx, v1_w), lambda b, h: (h, ctx0 + b, 0)),
        ],
        out_specs=pl.BlockSpec((nctx, V_HEAD), lambda b, h: (b, h)),
        out_shape=jax.ShapeDtypeStruct((batch * nctx, MLA_W), BF16),
        compiler_params=_cparams("parallel", "parallel"),
        name="attn_ctx",
    )(q, k, v1)
    return a_lat, a_ctx


def _gates_kernel(g_ref, par_ref, col_ref, row_ref):
    tb = g_ref.shape[0]
    z = g_ref[...] + par_ref[0:1, :]
    neg_a = -jnp.exp(par_ref[1:2, :])
    lane = lax.broadcasted_iota(jnp.int32, z.shape, 1)
    kind = (lane // 4) % 4
    is_ml = lane < 16
    is_cum = (kind % 2 == 1) & (lane < 32)
    is_bwd = kind >= 2
    soft = jnp.log(1.0 + jnp.exp(-jnp.abs(z)))
    log_sig = jnp.minimum(z, 0.0) - soft
    softplus = jnp.maximum(z, 0.0) + soft
    pre = jnp.where(is_ml, jnp.where(is_cum, log_sig, z), jnp.where(is_cum, neg_a * softplus, _sigmoid(z)))
    r = lax.broadcasted_iota(jnp.int32, (tb, tb), 0)
    c = lax.broadcasted_iota(jnp.int32, (tb, tb), 1)
    same = (r // CHUNK) == (c // CHUNK)
    lower = jnp.where(same & (c <= r), 1.0, 0.0).astype(F32)
    upper = jnp.where(same & (c >= r), 1.0, 0.0).astype(F32)
    cum_f = jnp.dot(lower, pre, preferred_element_type=F32, precision=lax.Precision.HIGHEST)
    cum_b = jnp.dot(upper, pre, preferred_element_type=F32, precision=lax.Precision.HIGHEST)
    out = jnp.where(is_cum, jnp.where(is_bwd, cum_b, cum_f), pre)
    is_mli = is_ml & jnp.logical_not(is_cum)
    d = out - pltpu.roll(out, P_GATE - ML_HEADS, 1)
    t = lax.broadcasted_iota(jnp.int32, z.shape, 0) % CHUNK
    run = d
    shift = 1
    while shift < CHUNK:
        prev_f = jnp.where(t >= shift, pltpu.roll(run, shift, 0), -jnp.inf)
        prev_b = jnp.where(t < CHUNK - shift, pltpu.roll(run, tb - shift, 0), -jnp.inf)
        run = jnp.maximum(run, jnp.where(is_bwd, prev_b, prev_f))
        shift *= 2
    col_ref[...] = jnp.where(is_mli, run, out)
    row_ref[...] = jnp.where(is_mli, d, out).T


def _gates(l, g, par):
    nt = g.shape[0]
    return pl.pallas_call(
        _gates_kernel,
        grid=(nt // TB,),
        in_specs=[
            pl.BlockSpec((TB, P_GATE), lambda i: (i, 0)),
            pl.BlockSpec((None, 8, P_GATE), lambda i: (l, 0, 0)),
        ],
        out_specs=[
            pl.BlockSpec((TB, P_GATE), lambda i: (i, 0)),
            pl.BlockSpec((P_GATE, TB), lambda i: (0, i)),
        ],
        out_shape=[
            jax.ShapeDtypeStruct((nt, P_GATE), F32),
            jax.ShapeDtypeStruct((P_GATE, nt), F32),
        ],
        compiler_params=_cparams("parallel"),
        name="gates",
    )(g, par)


def _scan_blocks(batch, seq, nctx):
    ncb = nctx // TB
    nlb = seq // TB
    ctx0 = batch * nlb

    def fwd(b, j):
        return jnp.where(j < ncb, ctx0 + b * ncb + j, b * nlb + (j - ncb))

    def bwd(b, j):
        return jnp.where(j < ncb, ctx0 + b * ncb + (ncb - 1 - j), b * nlb + (nlb - 1 - (j - ncb)))

    return ncb + nlb, fwd, bwd


def _dot(a, b, dims=NN_DIMS):
    return lax.dot_general(a.astype(BF16), b.astype(BF16), dims, preferred_element_type=F32)


def _scan_units(nheads):
    return [(d, h) for d in range(2) for h in range(nheads)]


def _mlstm_kernel(xf_ref, xb_ref, ktf_ref, ktb_ref, gcf_ref, gcb_ref, grf_ref, grb_ref, hf_ref, hb_ref, c_ref, m_ref):
    @pl.when(pl.program_id(1) == 0)
    def _():
        c_ref[...] = jnp.zeros_like(c_ref)
        m_ref[...] = jnp.full(m_ref.shape, M_INIT, F32)

    nch = TB // CHUNK
    row = lax.broadcasted_iota(jnp.int32, (CHUNK, CHUNK), 0)
    col = lax.broadcasted_iota(jnp.int32, (CHUNK, CHUNK), 1)
    kscale = ML_DK ** -0.5
    qw = ML_HEADS * ML_DK
    refs = ((xf_ref, ktf_ref, gcf_ref, grf_ref, hf_ref), (xb_ref, ktb_ref, gcb_ref, grb_ref, hb_ref))
    units = _scan_units(ML_HEADS)

    ones = jnp.ones((CHUNK, LANES), BF16)
    sel_row = lax.broadcasted_iota(jnp.int32, (P_GATE, 2 * ML_HEADS * LANES), 0)
    sel_blk = lax.broadcasted_iota(jnp.int32, (P_GATE, 2 * ML_HEADS * LANES), 1) // LANES
    loc = {}
    for ci in range(nch):
        for d in range(2):
            x_ref, kt_ref, gc_ref, gr_ref, _ = refs[d]
            ch = ci if d == 0 else nch - 1 - ci
            rs = slice(ch * CHUNK, (ch + 1) * CHUNK)
            src_col = jnp.where(sel_blk < ML_HEADS, d * 8 + ML_HEADS + sel_blk, d * 8 + sel_blk - ML_HEADS)
            select = jnp.where(sel_row == src_col, 1.0, 0.0).astype(BF16)
            g = gc_ref[rs, :]
            g_hi = g.astype(BF16)
            g_lo = (g - g_hi.astype(F32)).astype(BF16)
            rep = (jnp.dot(g_hi, select, preferred_element_type=F32)
                   + jnp.dot(g_lo, select, preferred_element_type=F32))
            mask = (col <= row) if d == 0 else (col >= row)
            last = CHUNK - 1 if d == 0 else 0
            for h in range(ML_HEADS):
                b_rep = rep[:, h * LANES:(h + 1) * LANES]
                pm_rep = rep[:, (ML_HEADS + h) * LANES:(ML_HEADS + h + 1) * LANES]
                d_row = gr_ref[d * 8 + h:d * 8 + h + 1, rs]
                b_end = b_rep[last:last + 1, :]
                loc[ci, d, h] = dict(
                    rs=rs, b_rep=b_rep, b_end=b_end,
                    log_d=jnp.where(mask, b_rep[:, :CHUNK] + d_row, -jnp.inf),
                    log_w=b_end[:, :CHUNK] + d_row,
                    q=x_ref[rs, h * ML_DK:(h + 1) * ML_DK],
                    kt=kt_ref[h * ML_DK:(h + 1) * ML_DK, rs],
                    v1=jnp.concatenate([x_ref[rs, 2 * qw + h * ML_DV:2 * qw + (h + 1) * ML_DV], ones], axis=-1),
                    row_max=b_rep + pm_rep,
                    w_max=b_end + pm_rep[last:last + 1, :])
    for u in loc.values():
        u["qk"] = _dot(u["q"], u["kt"])

    state = {(d, h): (c_ref[d * ML_HEADS + h], m_ref[d * ML_HEADS + h]) for d, h in units}
    for ci in range(nch):
        cur = [(dh, loc[(ci,) + dh], state[dh]) for dh in units]
        for _, u, (cn, m) in cur:
            log_inter = u["b_rep"] + m
            m_t = jnp.maximum(log_inter, u["row_max"])
            u["w_inter"] = jnp.exp(log_inter - m_t)
            u["floor"] = jnp.exp(-m_t)
            u["s"] = u["qk"] * jnp.exp(u["log_d"] - m_t[:, :CHUNK]) * kscale
        for _, u, (cn, m) in cur:
            u["qc"] = _dot(u["q"], cn)
            u["sv"] = _dot(u["s"], u["v1"])
        for (d, h), u, (cn, m) in cur:
            num = u["w_inter"] * u["qc"][:, :ML_DV] + u["sv"][:, :ML_DV]
            den = u["w_inter"] * u["qc"][:, ML_DV:] + u["sv"][:, ML_DV:]
            refs[d][4][u["rs"], h * ML_DV:(h + 1) * ML_DV] = num / jnp.maximum(jnp.abs(den), u["floor"])
            m_new = jnp.maximum(u["b_end"] + m, u["w_max"])
            u["decay"] = jnp.exp(u["b_end"] + m - m_new)
            u["m_new"] = m_new
            u["kwt"] = u["kt"].astype(F32) * (jnp.exp(u["log_w"] - m_new[:, :CHUNK]) * kscale)
        for _, u, _st in cur:
            u["upd"] = _dot(u["kwt"], u["v1"])
        for dh, u, (cn, m) in cur:
            state[dh] = (jnp.concatenate([u["decay"], u["decay"]], axis=-1) * cn + u["upd"], u["m_new"])
    for d, h in units:
        st = d * ML_HEADS + h
        c_ref[st], m_ref[st] = state[(d, h)]


def _mlstm(p_ml, kt, gcol, grow, batch, seq, nctx):
    nt = p_ml.shape[0]
    nsteps, fwd, bwd = _scan_blocks(batch, seq, nctx)
    xw = 2 * ML_HEADS * ML_DK + ML_W
    nst = 2 * ML_HEADS
    return pl.pallas_call(
        _mlstm_kernel,
        grid=(batch, nsteps),
        in_specs=[
            pl.BlockSpec((TB, xw), lambda b, j: (fwd(b, j), 0)),
            pl.BlockSpec((TB, xw), lambda b, j: (bwd(b, j), 0)),
            pl.BlockSpec((ML_HEADS * ML_DK, TB), lambda b, j: (0, fwd(b, j))),
            pl.BlockSpec((ML_HEADS * ML_DK, TB), lambda b, j: (0, bwd(b, j))),
            pl.BlockSpec((TB, P_GATE), lambda b, j: (fwd(b, j), 0)),
            pl.BlockSpec((TB, P_GATE), lambda b, j: (bwd(b, j), 0)),
            pl.BlockSpec((P_GATE, TB), lambda b, j: (0, fwd(b, j))),
            pl.BlockSpec((P_GATE, TB), lambda b, j: (0, bwd(b, j))),
        ],
        out_specs=[
            pl.BlockSpec((TB, ML_W), lambda b, j: (fwd(b, j), 0)),
            pl.BlockSpec((TB, ML_W), lambda b, j: (bwd(b, j), 0)),
        ],
        out_shape=[jax.ShapeDtypeStruct((nt, ML_W), F32)] * 2,
        scratch_shapes=[
            pltpu.VMEM((nst, ML_DK, ML_DV + LANES), F32),
            pltpu.VMEM((nst, 1, LANES), F32),
        ],
        compiler_params=_cparams("parallel", "arbitrary"),
        name="mlstm",
    )(p_ml, p_ml, kt, kt, gcol, gcol, grow, grow)


def _gdn_prep_kernel(x_ref, prev_ref, next_ref, w_ref, o_ref, buf_ref, *, seq_blocks, lat_blocks):
    i = pl.program_id(0)
    in_lat = i < lat_blocks
    first = jnp.where(in_lat, i % seq_blocks == 0, True)
    last = jnp.where(in_lat, i % seq_blocks == seq_blocks - 1, True)
    tb = x_ref.shape[0]
    prev = prev_ref[...].astype(F32)[HALO - 8:, :]
    nxt = next_ref[...].astype(F32)[:8, :]
    buf_ref[0:8, :] = jnp.where(first, 0.0, prev)
    buf_ref[8:8 + tb, :] = x_ref[...].astype(F32)
    buf_ref[8 + tb:16 + tb, :] = jnp.where(last, 0.0, nxt)
    half = CONV_W // 2
    y = None
    for j in range(CONV_W):
        term = buf_ref[8 - half + j:8 - half + j + tb, :] * w_ref[j:j + 1, :]
        y = term if y is None else y + term
    y = y * _sigmoid(y)
    nk = GD_HEADS * GD_DK
    for h in range(2 * GD_HEADS):
        cols = slice(h * GD_DK, (h + 1) * GD_DK)
        seg = y[:, cols]
        seg = seg * lax.rsqrt(jnp.sum(seg * seg, axis=-1, keepdims=True) + EPS)
        if h < GD_HEADS:
            seg = seg * (GD_DK ** -0.5)
        o_ref[:, cols] = seg
    o_ref[:, 2 * nk:] = y[:, 2 * nk:]


def _gdn_prep(l, p_gd, conv_w, batch, seq, nctx):
    nt = p_gd.shape[0]
    cw = 2 * GD_HEADS * GD_DK + GD_W
    tb = nctx
    hb = tb // HALO
    nblk = nt // tb
    kern = functools.partial(_gdn_prep_kernel, seq_blocks=seq // tb, lat_blocks=batch * seq // tb)
    return pl.pallas_call(
        kern,
        grid=(nblk,),
        in_specs=[
            pl.BlockSpec((tb, cw), lambda i: (i, 0)),
            pl.BlockSpec((HALO, cw), lambda i: (jnp.maximum(i * hb - 1, 0), 0)),
            pl.BlockSpec((HALO, cw), lambda i: (jnp.minimum((i + 1) * hb, nblk * hb - 1), 0)),
            pl.BlockSpec((None, CONV_W, cw), lambda i: (l, 0, 0)),
        ],
        out_specs=pl.BlockSpec((tb, cw), lambda i: (i, 0)),
        out_shape=jax.ShapeDtypeStruct((nt, cw), F32),
        scratch_shapes=[pltpu.VMEM((tb + 16, cw), F32)],
        compiler_params=_cparams("parallel"),
        name="gdn_prep",
    )(p_gd, p_gd, p_gd, conv_w)


INV_BLOCK = 16


def _unit_triangular_inverses(mats):
    n = mats[0].shape[0]
    row = lax.broadcasted_iota(jnp.int32, (n, n), 0)
    col = lax.broadcasted_iota(jnp.int32, (n, n), 1)
    eye = jnp.where(row == col, 1.0, 0.0)
    same_block = row // INV_BLOCK == col // INV_BLOCK
    ps = [jnp.where(same_block, a, 0.0) for a in mats]
    xs = [eye - p for p in ps]
    for _ in range((INV_BLOCK - 1).bit_length() - 1):
        ps = [_dot(p, p) for p in ps]
        xs = [x + xp for x, xp in zip(xs, [_dot(x, p) for x, p in zip(xs, ps)])]
    size = INV_BLOCK
    while size < n:
        sel = (row // (2 * size) == col // (2 * size)) & (row // size != col // size)
        ts = [_dot(x, jnp.where(sel, a, 0.0)) for x, a in zip(xs, mats)]
        xs = [x - tx for x, tx in zip(xs, [_dot(t, x) for t, x in zip(ts, xs)])]
        size *= 2
    return xs


def _gdn_kernel(xf_ref, xb_ref, gcf_ref, gcb_ref, grf_ref, grb_ref, of_ref, ob_ref, s_ref):
    @pl.when(pl.program_id(1) == 0)
    def _():
        s_ref[...] = jnp.zeros_like(s_ref)

    nch = TB // CHUNK
    row = lax.broadcasted_iota(jnp.int32, (CHUNK, CHUNK), 0)
    col = lax.broadcasted_iota(jnp.int32, (CHUNK, CHUNK), 1)
    nk = GD_HEADS * GD_DK
    refs = ((xf_ref, gcf_ref, grf_ref, of_ref), (xb_ref, gcb_ref, grb_ref, ob_ref))
    units = _scan_units(GD_HEADS)

    loc = {}
    for ci in range(nch):
        for d, h in units:
            x_ref, gc_ref, gr_ref, _ = refs[d]
            ch = ci if d == 0 else nch - 1 - ci
            rs = slice(ch * CHUNK, (ch + 1) * CHUNK)
            cb = 16 + d * 8 + h
            cg = 16 + d * 8 + 4 + h
            beta = jnp.broadcast_to(gc_ref[rs, cb:cb + 1], (CHUNK, LANES))
            g_col = jnp.broadcast_to(gc_ref[rs, cg:cg + 1], (CHUNK, LANES))
            g_row = gr_ref[cg:cg + 1, rs]
            g_end = g_col[CHUNK - 1:CHUNK, :] if d == 0 else g_col[0:1, :]
            incl = (col <= row) if d == 0 else (col >= row)
            q = x_ref[rs, h * GD_DK:(h + 1) * GD_DK]
            k = x_ref[rs, nk + h * GD_DK:nk + (h + 1) * GD_DK]
            v = x_ref[rs, 2 * nk + h * GD_DV:2 * nk + (h + 1) * GD_DV]
            kb = k * beta
            e_col = jnp.exp(g_col)
            loc[ci, d, h] = dict(
                rs=rs, k=k, kb=kb, q=q,
                strict=(col < row) if d == 0 else (col > row),
                decay=jnp.exp(jnp.where(incl, g_col[:, :CHUNK] - g_row, -jnp.inf)),
                rhs=jnp.concatenate([v * beta, kb * e_col], axis=-1),
                qg=q * e_col, kd=k * jnp.exp(g_end - g_col), e_end=jnp.exp(g_end))
    us = list(loc.values())
    for u in us:
        kk_qk = _dot(jnp.concatenate([u["kb"], u["q"]], axis=0), u["k"], NT_DIMS)
        u["kk"] = kk_qk[:CHUNK]
        u["attn"] = kk_qk[CHUNK:] * u["decay"]
    tinvs = _unit_triangular_inverses([jnp.where(u["strict"], u["kk"] * u["decay"], 0.0) for u in us])
    for u, tinv in zip(us, tinvs):
        u["uw"] = _dot(tinv, u["rhs"])

    state = {(d, h): s_ref[d * GD_HEADS + h] for d, h in units}
    for ci in range(nch):
        cur = [(dh, loc[(ci,) + dh]) for dh in units]
        for dh, u in cur:
            u["ws_qs"] = _dot(jnp.concatenate([u["uw"][:, GD_DV:], u["qg"]], axis=0), state[dh])
        for dh, u in cur:
            u["v_new"] = u["uw"][:, :GD_DV] - u["ws_qs"][:CHUNK]
            u["qs"] = u["ws_qs"][CHUNK:]
        for dh, u in cur:
            u["av"] = _dot(u["attn"], u["v_new"])
            u["upd"] = _dot(u["kd"], u["v_new"], TN_DIMS)
        for (d, h), u in cur:
            refs[d][3][u["rs"], h * GD_DV:(h + 1) * GD_DV] = u["qs"] + u["av"]
            state[(d, h)] = state[(d, h)] * u["e_end"] + u["upd"]
    for d, h in units:
        s_ref[d * GD_HEADS + h] = state[(d, h)]


def _gdn(qkv, gcol, grow, batch, seq, nctx):
    nt = qkv.shape[0]
    nsteps, fwd, bwd = _scan_blocks(batch, seq, nctx)
    cw = qkv.shape[1]
    return pl.pallas_call(
        _gdn_kernel,
        grid=(batch, nsteps),
        in_specs=[
            pl.BlockSpec((TB, cw), lambda b, j: (fwd(b, j), 0)),
            pl.BlockSpec((TB, cw), lambda b, j: (bwd(b, j), 0)),
            pl.BlockSpec((TB, P_GATE), lambda b, j: (fwd(b, j), 0)),
            pl.BlockSpec((TB, P_GATE), lambda b, j: (bwd(b, j), 0)),
            pl.BlockSpec((P_GATE, TB), lambda b, j: (0, fwd(b, j))),
            pl.BlockSpec((P_GATE, TB), lambda b, j: (0, bwd(b, j))),
        ],
        out_specs=[
            pl.BlockSpec((TB, GD_W), lambda b, j: (fwd(b, j), 0)),
            pl.BlockSpec((TB, GD_W), lambda b, j: (bwd(b, j), 0)),
        ],
        out_shape=[jax.ShapeDtypeStruct((nt, GD_W), F32)] * 2,
        scratch_shapes=[pltpu.VMEM((2 * GD_HEADS, GD_DK, GD_DV), F32)],
        compiler_params=_cparams("parallel", "arbitrary"),
        name="gdn",
    )(qkv, qkv, gcol, gcol, grow, grow)


def _head_rms(y, gain, nheads, width):
    parts = []
    for h in range(nheads):
        seg = y[:, h * width:(h + 1) * width]
        parts.append(_rms(seg))
    return jnp.concatenate(parts, axis=-1) * gain


def _merge_kernel(*refs, n_x, lat_tiles):
    x_refs = refs[:n_x]
    (mod_ref, al_ref, ac_ref, hf_ref, hb_ref, o_ref, gf_ref, gb_ref, z_ref,
     an_ref, mn_ref, gn_ref, w_ref, out_ref) = refs[n_x:]
    a = _token_tile((al_ref, ac_ref), lat_tiles).astype(F32)
    ya = _head_rms(a, an_ref[...], MLA_HEADS, V_HEAD)
    ym = _head_rms(hf_ref[...] + hb_ref[...], mn_ref[...], ML_HEADS, ML_DV) * _sigmoid(o_ref[...].astype(F32))
    z = z_ref[...].astype(F32)
    yg = _head_rms(gf_ref[...] + gb_ref[...], gn_ref[...], GD_HEADS, GD_DV) * (z * _sigmoid(z))
    y = jnp.dot(ya.astype(BF16), w_ref[0:MLA_W, :], preferred_element_type=F32)
    y = y + jnp.dot(ym.astype(BF16), w_ref[MLA_W:MLA_W + ML_W, :], preferred_element_type=F32)
    y = y + jnp.dot(yg.astype(BF16), w_ref[MLA_W + ML_W:, :], preferred_element_type=F32)
    out_ref[...] = _token_tile(x_refs, lat_tiles) + mod_ref[2:3, :] * y


def _merge(l, xs, mod, a_lat, a_ctx, hf, hb, p_ml, gf, gb, p_gd, mla_norm, ml_norm, gd_norm, w_out, seq, batch, nrows):
    d = xs[0].shape[1]
    lat_tiles = a_lat.shape[0] // TM
    assert a_ctx.shape[0] == TM
    row = functools.partial(_mod_row, seq=seq, batch=batch)
    tile = lambda w, cb=0: pl.BlockSpec((TM, w), lambda i: (i, cb))
    vec = lambda w: pl.BlockSpec((None, 1, w), lambda i: (l, 0, 0))
    return pl.pallas_call(
        functools.partial(_merge_kernel, n_x=len(xs), lat_tiles=lat_tiles),
        grid=(nrows // TM,),
        in_specs=_token_specs(xs, d) + [
            pl.BlockSpec((None, None, 6, d), lambda i: (l, row(i), 0, 0)),
            *_token_specs((a_lat, a_ctx), MLA_W),
            tile(ML_W), tile(ML_W), tile(ML_W, 2),
            tile(GD_W), tile(GD_W), tile(GD_W, 3),
            vec(MLA_W), vec(ML_W), vec(GD_W),
            pl.BlockSpec((None, MIX_W, d), lambda i: (l, 0, 0)),
        ],
        out_specs=tile(d),
        out_shape=jax.ShapeDtypeStruct((nrows, d), F32),
        compiler_params=_cparams("parallel"),
        name="merge",
    )(*xs, mod, a_lat, a_ctx, hf, hb, p_ml, gf, gb, p_gd, mla_norm[:, None, :], ml_norm[:, None, :],
      jnp.tile(gd_norm, (1, GD_HEADS))[:, None, :], w_out)


def _mlp_kernel(x_ref, mod_ref, nw_ref, w1_ref, w2_ref, *rest):
    out_ref, h_ref, acc_ref = rest[-3:]
    j = pl.program_id(1)

    @pl.when(j == 0)
    def _():
        h_ref[...] = _norm_mod(x_ref[...], nw_ref[...], mod_ref[3:4, :], mod_ref[4:5, :]).astype(BF16)
        acc_ref[...] = jnp.zeros_like(acc_ref)

    a = jnp.maximum(jnp.dot(h_ref[...], w1_ref[...], preferred_element_type=F32), 0.0)
    acc_ref[...] += jnp.dot((a * a).astype(BF16), w2_ref[...], preferred_element_type=F32)

    @pl.when(j == pl.num_programs(1) - 1)
    def _():
        y = x_ref[...] + mod_ref[5:6, :] * acc_ref[...]
        out_ref[...] = _rms(y) * rest[0][...] if len(rest) == 4 else y


def _mlp(l, xg, mod, norm_w, w1, w2, seq, batch, nrows, final_w=None):
    d = xg.shape[1]
    f = w1.shape[-1]
    tf = min(MLP_TF, f)
    row = functools.partial(_mod_row, seq=seq, batch=batch)
    extra = [] if final_w is None else [final_w.reshape(1, d)]
    return pl.pallas_call(
        _mlp_kernel,
        grid=(nrows // TM, f // tf),
        in_specs=[
            pl.BlockSpec((TM, d), lambda i, j: (i, 0)),
            pl.BlockSpec((None, None, 6, d), lambda i, j: (l, row(i), 0, 0)),
            pl.BlockSpec((None, 1, d), lambda i, j: (l, 0, 0)),
            pl.BlockSpec((None, d, tf), lambda i, j: (l, 0, j)),
            pl.BlockSpec((None, tf, d), lambda i, j: (l, j, 0)),
        ] + [pl.BlockSpec((1, d), lambda i, j: (0, 0)) for _ in extra],
        out_specs=pl.BlockSpec((TM, d), lambda i, j: (i, 0)),
        out_shape=jax.ShapeDtypeStruct((nrows, d), F32),
        scratch_shapes=[pltpu.VMEM((TM, d), BF16), pltpu.VMEM((TM, d), F32)],
        compiler_params=_cparams("parallel", "arbitrary"),
        name="mlp",
    )(xg, mod, norm_w[:, None, :], w1, w2, *extra)


def _regroup_w_in_kernel(w_ref, o_ref):
    o_ml = Q_LORA + KV_LORA + QK_ROPE
    o_mlg = o_ml + P_ML
    o_gd = o_mlg + 4 * ML_HEADS
    o_gdg = o_gd + P_GD
    rows = w_ref.shape[0]
    zeros = lambda n: jnp.zeros((rows, n), BF16)
    piece = lambda a, b: w_ref[:, a:b].astype(BF16)
    o_ref[...] = jnp.concatenate([
        piece(0, o_ml), zeros(P_MLA - o_ml),
        piece(o_ml, o_mlg),
        piece(o_gd, o_gdg),
        piece(o_mlg, o_gd), piece(o_gdg, o_gdg + 4 * GD_HEADS), zeros(P_GATE - 4 * ML_HEADS - 4 * GD_HEADS),
    ], axis=-1)


def _regroup_w_in(w_in):
    nl, d, n_in = w_in.shape
    tr = 256
    return pl.pallas_call(
        _regroup_w_in_kernel,
        grid=(nl, d // tr),
        in_specs=[pl.BlockSpec((None, tr, n_in), lambda l, i: (l, i, 0))],
        out_specs=pl.BlockSpec((None, tr, P_ALL), lambda l, i: (l, i, 0)),
        out_shape=jax.ShapeDtypeStruct((nl, d, P_ALL), BF16),
        compiler_params=_cparams("parallel", "parallel"),
        name="regroup_w_in",
    )(w_in)


def _regroup_heads(w, first):
    nl, kdim, _ = w.shape
    w4 = w.reshape(nl, kdim, MLA_HEADS, -1)
    return jnp.concatenate([w4[..., :first].reshape(nl, kdim, -1), w4[..., first:].reshape(nl, kdim, -1)],
                           axis=-1).astype(BF16)


def _rope_tables(seq):
    half = QK_ROPE // 2
    t = jnp.arange(seq, dtype=jnp.int32)
    inv = ROPE_BASE ** (-jnp.arange(0, half, 2, dtype=F32) / half)
    ang_r = (t // GRID_W).astype(F32)[:, None] * inv
    ang_c = (t % GRID_W).astype(F32)[:, None] * inv
    cos = jnp.concatenate([jnp.cos(ang_r)] * 2 + [jnp.cos(ang_c)] * 2, axis=-1)
    sin = jnp.concatenate([-jnp.sin(ang_r), jnp.sin(ang_r), -jnp.sin(ang_c), jnp.sin(ang_c)], axis=-1)
    cos = jnp.concatenate([cos, jnp.ones((TM, QK_ROPE), F32)], axis=0)
    sin = jnp.concatenate([sin, jnp.zeros((TM, QK_ROPE), F32)], axis=0)
    return jnp.tile(cos, (1, MLA_HEADS)), jnp.tile(sin, (1, MLA_HEADS))


def _gate_params(ml_gate_bias, gd_a_log, gd_dt_bias):
    nl = ml_gate_bias.shape[0]
    zh = jnp.zeros((nl, GD_HEADS), F32)
    bias = jnp.concatenate([ml_gate_bias, zh, gd_dt_bias[:, 0], zh, gd_dt_bias[:, 1]], axis=-1)
    alog = jnp.concatenate([jnp.zeros((nl, 4 * ML_HEADS), F32), zh, gd_a_log[:, 0], zh, gd_a_log[:, 1]], axis=-1)
    par = jnp.stack([bias, alog], axis=1)
    return jnp.pad(par, ((0, 0), (0, 6), (0, P_GATE - par.shape[-1])))


def kernel(x, c, ctx, c_ctx, w_ada, b_ada, norm1, norm2, w_in, mla_q_norm, mla_w_uq, mla_kv_norm, mla_w_ukv, mla_out_norm, ml_gate_bias, ml_out_norm, gd_conv, gd_a_log, gd_dt_bias, gd_out_norm, w_out, w_mlp1, w_mlp2, final_norm):
    batch, seq, d = x.shape
    nctx = ctx.shape[1]
    depth = w_ada.shape[0]
    n_lat = batch * seq
    n_all = n_lat + batch * nctx
    assert seq % TM == 0 and (batch * nctx) % TM == 0 and nctx % TB == 0 and TM % nctx == 0

    xs = (x.reshape(n_lat, d), ctx.reshape(batch * nctx, d))
    mod_all = _ada(jnp.concatenate([c, c_ctx[None]], axis=0), w_ada, b_ada).reshape(depth, batch + 1, 6, d)
    w_in_g = _regroup_w_in(w_in)
    wuq_g = _regroup_heads(mla_w_uq, QK_NOPE)
    wukv_g = _regroup_heads(mla_w_ukv, QK_NOPE)
    w_out_b = w_out.astype(BF16)
    w1_b = w_mlp1.astype(BF16)
    w2_b = w_mlp2.astype(BF16)
    cos_t, sin_t = _rope_tables(seq)
    gate_par = _gate_params(ml_gate_bias, gd_a_log, gd_dt_bias)

    for l in range(depth):
        last = l == depth - 1
        p_mla, p_ml, p_gd, p_gate, ml_kt = _inproj(l, xs, mod_all, norm1, w_in_g, seq, batch)
        q, k, v1 = _mla_qkv_call(l, p_mla, mla_q_norm, mla_kv_norm, wuq_g, wukv_g, cos_t, sin_t, seq, n_lat)
        a_lat, a_ctx = _attention(q, k, v1, batch, seq, nctx)
        gcol, grow = _gates(l, p_gate, gate_par)
        hf, hb = _mlstm(p_ml, ml_kt, gcol, grow, batch, seq, nctx)
        gd_qkv = _gdn_prep(l, p_gd, gd_conv, batch, seq, nctx)
        gf, gb = _gdn(gd_qkv, gcol, grow, batch, seq, nctx)
        nrows = n_lat if last else n_all
        xg = _merge(l, xs, mod_all, a_lat, a_ctx, hf, hb, p_ml, gf, gb, p_gd, mla_out_norm, ml_out_norm,
                    gd_out_norm, w_out_b, seq, batch, nrows)
        xs = (_mlp(l, xg, mod_all, norm2, w1_b, w2_b, seq, batch, nrows, final_norm if last else None),)
    return xs[0].reshape(batch, seq, d)
```

```python
import functools

import jax
import jax.numpy as jnp
from jax import lax
from jax.experimental import pallas as pl
from jax.experimental.pallas import tpu as pltpu

F32 = jnp.float32
BF16 = jnp.bfloat16

EPS = 1e-6
LOG2_E = 1.4426950408889634
M_INIT = -1e30
GRID_W = 64
ROPE_BASE = 10000.0

MLA_HEADS = 8
Q_LORA = 512
KV_LORA = 256
QK_NOPE = 128
QK_ROPE = 64
V_HEAD = 128
QK_DIM = QK_NOPE + QK_ROPE
ML_HEADS = 4
ML_DK = 64
ML_DV = 128
GD_HEADS = 4
GD_DK = 128
GD_DV = 128
CHUNK = 64
CONV_W = 5

MLA_W = MLA_HEADS * V_HEAD
ML_W = ML_HEADS * ML_DV
GD_W = GD_HEADS * GD_DV
MIX_W = MLA_W + ML_W + GD_W

P_MLA = 896
P_ML = 1536
P_GD = 2048
P_GATE = 128
P_ALL = P_MLA + P_ML + P_GD + P_GATE

TM = 512
MLP_TF = 1024
ADA_COLS = 1024
TB = 256
HALO = 16
GATE_ROWS = 256
LANES = 128

VMEM_LIMIT_V7X = 56 * 1024 * 1024

NT_DIMS = (((1,), (1,)), ((), ()))
TN_DIMS = (((0,), (0,)), ((), ()))
NN_DIMS = (((1,), (0,)), ((), ()))


def _cparams(*sem):
    return pltpu.CompilerParams(dimension_semantics=sem, vmem_limit_bytes=VMEM_LIMIT_V7X)


def _sigmoid(x):
    return 1.0 / (1.0 + jnp.exp(-x))


def _rms(x):
    return x * lax.rsqrt(jnp.mean(x * x, axis=-1, keepdims=True) + EPS)


def _norm_mod(x, norm_w, shift, scale):
    return _rms(x) * norm_w * (1.0 + scale) + shift


def _mod_row(i, seq, batch):
    return jnp.minimum((i * TM) // seq, batch)


def _ada_kernel(cb_ref, w_ref, b_ref, o_ref, s_ref):
    @pl.when((pl.program_id(0) == 0) & (pl.program_id(1) == 0))
    def _():
        cb = cb_ref[...]
        s_ref[...] = cb * _sigmoid(cb)

    rows, d = cb_ref.shape[0], cb_ref.shape[1]
    tn = w_ref.shape[-1]
    tc = next(t for t in (ADA_COLS, ADA_COLS // 2, LANES) if tn % t == 0)
    sub = 8
    for c0 in range(0, tn, tc):
        def body(kb, accs):
            k0 = pl.multiple_of(kb * sub, sub)
            w = w_ref[pl.ds(k0, sub), c0:c0 + tc]
            return tuple(acc + w * jnp.tile(s_ref[r, pl.ds(k0, sub), :], (1, tc // LANES))
                         for r, acc in enumerate(accs))

        accs = lax.fori_loop(0, d // sub, body, tuple(jnp.zeros((sub, tc), F32) for _ in range(rows)), unroll=4)
        for r in range(rows):
            o_ref[r:r + 1, c0:c0 + tc] = jnp.sum(accs[r], axis=0, keepdims=True) + b_ref[:, c0:c0 + tc]


def _ada(cvec, w_ada, b_ada):
    nl, d, n6 = w_ada.shape
    r = cvec.shape[0]
    tn = next(t for t in (2048, 1536, 1024, 512, LANES) if n6 % t == 0)
    cb = jnp.broadcast_to(cvec[:, :, None], (r, d, LANES))
    return pl.pallas_call(
        _ada_kernel,
        grid=(nl, n6 // tn),
        in_specs=[
            pl.BlockSpec((r, d, LANES), lambda l, j: (0, 0, 0)),
            pl.BlockSpec((None, d, tn), lambda l, j: (l, 0, j)),
            pl.BlockSpec((None, 1, tn), lambda l, j: (l, 0, j)),
        ],
        out_specs=pl.BlockSpec((None, r, tn), lambda l, j: (l, 0, j)),
        out_shape=jax.ShapeDtypeStruct((nl, r, n6), F32),
        scratch_shapes=[pltpu.VMEM((r, d, LANES), F32)],
        compiler_params=_cparams("arbitrary", "arbitrary"),
        name="ada",
    )(cb, w_ada, b_ada.reshape(nl, 1, n6))


def _token_specs(xs, width):
    if len(xs) == 1:
        return [pl.BlockSpec((TM, width), lambda i: (i, 0))]
    nl = xs[0].shape[0] // TM
    return [pl.BlockSpec((TM, width), lambda i: (jnp.minimum(i, nl - 1), 0)),
            pl.BlockSpec((TM, width), lambda i: (jnp.maximum(i - nl, 0), 0))]


def _token_tile(x_refs, lat_tiles):
    if len(x_refs) == 1:
        return x_refs[0][...]
    return jnp.where(pl.program_id(0) < lat_tiles, x_refs[0][...], x_refs[1][...])


def _gate_columns(g, bias, a_log):
    tb = g.shape[0]
    z = g + bias
    neg_a = -jnp.exp(a_log)
    lane = lax.broadcasted_iota(jnp.int32, z.shape, 1)
    kind = (lane // 4) % 4
    is_ml = lane < 16
    is_cum = (kind % 2 == 1) & (lane < 32)
    is_bwd = kind >= 2
    soft = jnp.log(1.0 + jnp.exp(-jnp.abs(z)))
    log_sig = jnp.minimum(z, 0.0) - soft
    softplus = jnp.maximum(z, 0.0) + soft
    pre = jnp.where(is_ml, jnp.where(is_cum, log_sig, z), jnp.where(is_cum, neg_a * softplus, _sigmoid(z)))
    r = lax.broadcasted_iota(jnp.int32, (tb, tb), 0)
    c = lax.broadcasted_iota(jnp.int32, (tb, tb), 1)
    same = (r // CHUNK) == (c // CHUNK)
    lower = jnp.where(same & (c <= r), 1.0, 0.0).astype(BF16)
    upper = jnp.where(same & (c >= r), 1.0, 0.0).astype(BF16)
    hi = pre.astype(BF16)
    rest = pre - hi.astype(F32)
    mid = rest.astype(BF16)
    parts = (hi, mid, (rest - mid.astype(F32)).astype(BF16))
    cum_f = sum(jnp.dot(lower, p, preferred_element_type=F32) for p in parts)
    cum_b = sum(jnp.dot(upper, p, preferred_element_type=F32) for p in parts)
    out = jnp.where(is_cum, jnp.where(is_bwd, cum_b, cum_f), pre)
    is_mli = is_ml & jnp.logical_not(is_cum)
    d = out - pltpu.roll(out, P_GATE - ML_HEADS, 1)
    t = lax.broadcasted_iota(jnp.int32, z.shape, 0) % CHUNK
    run = d
    shift = 1
    while shift < CHUNK:
        prev_f = jnp.where(t >= shift, pltpu.roll(run, shift, 0), -jnp.inf)
        prev_b = jnp.where(t < CHUNK - shift, pltpu.roll(run, tb - shift, 0), -jnp.inf)
        run = jnp.maximum(run, jnp.where(is_bwd, prev_b, prev_f))
        shift *= 2
    return jnp.where(is_mli, run, out), jnp.where(is_mli, d, out).T


def _short_conv_heads(buf_ref, cw_ref, tap_ok):
    half = CONV_W // 2
    y = None
    for j in range(CONV_W):
        term = buf_ref[HALO - half + j:HALO - half + j + TM, :] * cw_ref[j:j + 1, :]
        if tap_ok[j] is not None:
            term = term * tap_ok[j]
        y = term if y is None else y + term
    y = y * _sigmoid(y)
    nk = GD_HEADS * GD_DK
    parts = []
    for h in range(2 * GD_HEADS):
        seg = y[:, h * GD_DK:(h + 1) * GD_DK]
        seg = seg * lax.rsqrt(jnp.sum(seg * seg, axis=-1, keepdims=True) + EPS)
        parts.append(seg * (GD_DK ** -0.5) if h < GD_HEADS else seg)
    return jnp.concatenate(parts + [y[:, 2 * nk:]], axis=-1)


def _inproj_kernel(*refs, n_x, lat_tiles, seq_tiles, nctx):
    x_refs = refs[:n_x]
    (xp_ref, xn_ref, mod_ref, nw_ref, w_ref, gpar_ref, cw_ref,
     omla_ref, oml_ref, kt_ref, gdq_ref, z_ref, gcol_ref, grow_ref, buf_ref) = refs[n_x:]
    i = pl.program_id(0)
    in_lat = i < lat_tiles
    first = jnp.logical_or(jnp.logical_not(in_lat), i % seq_tiles == 0)
    last = jnp.logical_or(jnp.logical_not(in_lat), i % seq_tiles == seq_tiles - 1)

    x_ext = jnp.concatenate([xp_ref[...], _token_tile(x_refs, lat_tiles), xn_ref[...]], axis=0)
    h_ext = _norm_mod(x_ext, nw_ref[...], mod_ref[0:1, :], mod_ref[1:2, :]).astype(BF16)
    h = h_ext[HALO:HALO + TM]
    proj = lambda lhs, c0, n: jnp.dot(lhs, w_ref[:, c0:c0 + n], preferred_element_type=F32)

    c_gd = P_MLA + P_ML
    qkv_w = 2 * GD_HEADS * GD_DK + GD_W
    p_qkv = proj(h_ext, c_gd, qkv_w)
    buf_ref[0:HALO, :] = jnp.where(first, 0.0, p_qkv[0:HALO])
    buf_ref[HALO:HALO + TM, :] = p_qkv[HALO:HALO + TM]
    buf_ref[HALO + TM:, :] = jnp.where(last, 0.0, p_qkv[HALO + TM:])
    gates = proj(h, c_gd + P_GD, P_GATE)
    omla_ref[...] = proj(h, 0, P_MLA).astype(omla_ref.dtype)
    p_ml = proj(h, P_MLA, P_ML)
    oml_ref[...] = p_ml.astype(oml_ref.dtype)
    kw = ML_HEADS * ML_DK
    kt_ref[...] = p_ml[:, kw:2 * kw].T.astype(kt_ref.dtype)
    z_ref[...] = proj(h, c_gd + qkv_w, GD_W).astype(z_ref.dtype)

    pos = lax.broadcasted_iota(jnp.int32, (TM, LANES), 0) % nctx
    reps = qkv_w // LANES
    tap_ok = []
    for j in range(CONV_W):
        o = j - CONV_W // 2
        crosses = (pos + o < 0) | (pos + o >= nctx)
        keep = jnp.where(jnp.logical_or(in_lat, jnp.logical_not(crosses)), 1.0, 0.0)
        tap_ok.append(None if o == 0 or TM <= nctx else jnp.tile(keep, (1, reps)))
    gdq_ref[...] = _short_conv_heads(buf_ref, cw_ref, tap_ok).astype(gdq_ref.dtype)
    for r0 in range(0, TM, GATE_ROWS):
        col, row_t = _gate_columns(gates[r0:r0 + GATE_ROWS], gpar_ref[0:1, :], gpar_ref[1:2, :])
        gcol_ref[r0:r0 + GATE_ROWS, :] = col
        grow_ref[:, r0:r0 + GATE_ROWS] = row_t


def _inproj(l, xs, mod, norm_w, w, gate_par, conv_w, seq, batch, nctx):
    nt = sum(x.shape[0] for x in xs)
    d = xs[0].shape[1]
    row = functools.partial(_mod_row, seq=seq, batch=batch)
    lat_tiles = batch * seq // TM
    hpt = TM // HALO
    n_halo = xs[0].shape[0] // HALO
    qkv_w = 2 * GD_HEADS * GD_DK + GD_W
    kw = ML_HEADS * ML_DK
    tile = lambda n: pl.BlockSpec((TM, n), lambda i: (i, 0))
    outs = [
        (tile(P_MLA), (nt, P_MLA), BF16),
        (tile(P_ML), (nt, P_ML), BF16),
        (pl.BlockSpec((kw, TM), lambda i: (0, i)), (kw, nt), BF16),
        (tile(qkv_w), (nt, qkv_w), BF16),
        (tile(GD_W), (nt, GD_W), BF16),
        (tile(P_GATE), (nt, P_GATE), F32),
        (pl.BlockSpec((P_GATE, TM), lambda i: (0, i)), (P_GATE, nt), F32),
    ]
    kern = functools.partial(_inproj_kernel, n_x=len(xs), lat_tiles=lat_tiles, seq_tiles=seq // TM, nctx=nctx)
    return pl.pallas_call(
        kern,
        grid=(nt // TM,),
        in_specs=_token_specs(xs, d) + [
            pl.BlockSpec((HALO, d), lambda i: (jnp.clip(i * hpt - 1, 0, n_halo - 1), 0)),
            pl.BlockSpec((HALO, d), lambda i: (jnp.clip((i + 1) * hpt, 0, n_halo - 1), 0)),
            pl.BlockSpec((None, None, 6, d), lambda i: (l, row(i), 0, 0)),
            pl.BlockSpec((None, 1, d), lambda i: (l, 0, 0)),
            pl.BlockSpec((None, d, P_ALL), lambda i: (l, 0, 0), pipeline_mode=pl.Buffered(1)),
            pl.BlockSpec((None, 8, P_GATE), lambda i: (l, 0, 0)),
            pl.BlockSpec((None, CONV_W, qkv_w), lambda i: (l, 0, 0)),
        ],
        out_specs=[o[0] for o in outs],
        out_shape=[jax.ShapeDtypeStruct(o[1], o[2]) for o in outs],
        scratch_shapes=[pltpu.VMEM((TM + 2 * HALO, qkv_w), F32)],
        compiler_params=_cparams("parallel"),
        name="inproj",
    )(*xs, xs[0], xs[0], mod, norm_w[:, None, :], w, gate_par, conv_w)


def _swap16(x):
    n = x.shape[-1]
    lane = lax.broadcasted_iota(jnp.int32, x.shape, 1)
    return jnp.where(lane % 32 < 16, pltpu.roll(x, n - 16, 1), pltpu.roll(x, 16, 1))


def _mla_qkv_kernel(p_ref, qn_ref, kvn_ref, wuq_ref, wukv_ref, cos_ref, sin_ref, q_ref, k_ref, v_ref):
    p = p_ref[...].astype(F32)
    cq = p[:, :Q_LORA]
    ckv = p[:, Q_LORA:Q_LORA + KV_LORA]
    kpe = p[:, Q_LORA + KV_LORA:]
    cos = cos_ref[...]
    sin = sin_ref[...]
    qn = (_rms(cq) * qn_ref[...]).astype(BF16)
    q = jnp.dot(qn, wuq_ref[...], preferred_element_type=F32) * (QK_DIM ** -0.5 * LOG2_E)
    nope_w = MLA_HEADS * QK_NOPE
    qpe = q[:, nope_w:]
    qpe = qpe * cos + _swap16(qpe) * sin
    kvn = (_rms(ckv) * kvn_ref[...]).astype(BF16)
    kv = jnp.dot(kvn, wukv_ref[...], preferred_element_type=F32)
    kpe = kpe * cos[:, :LANES] + _swap16(kpe) * sin[:, :LANES]
    kpe = kpe[:, :QK_ROPE]
    ones = jnp.ones((q.shape[0], LANES), BF16)
    for h in range(MLA_HEADS):
        qh = jnp.concatenate([q[:, h * QK_NOPE:(h + 1) * QK_NOPE], qpe[:, h * QK_ROPE:(h + 1) * QK_ROPE]], axis=-1)
        kh = jnp.concatenate([kv[:, h * QK_NOPE:(h + 1) * QK_NOPE], kpe], axis=-1)
        q_ref[h] = qh.astype(BF16)
        k_ref[h] = kh.astype(BF16)
        v_ref[h, :, :V_HEAD] = kv[:, nope_w + h * V_HEAD:nope_w + (h + 1) * V_HEAD].astype(BF16)
        v_ref[h, :, V_HEAD:] = ones


def _mla_qkv_call(l, p_mla, q_norm, kv_norm, wuq, wukv, cos_t, sin_t, seq, n_lat_rows):
    nt = p_mla.shape[0]
    lat_tiles = seq // TM
    nl_tiles = n_lat_rows // TM
    pe_w = MLA_HEADS * QK_ROPE
    v1_w = V_HEAD + LANES

    def rope_block(i):
        return (jnp.where(i < nl_tiles, i % lat_tiles, lat_tiles), 0)

    return pl.pallas_call(
        _mla_qkv_kernel,
        grid=(nt // TM,),
        in_specs=[
            pl.BlockSpec((TM, P_MLA), lambda i: (i, 0)),
            pl.BlockSpec((None, 1, Q_LORA), lambda i: (l, 0, 0)),
            pl.BlockSpec((None, 1, KV_LORA), lambda i: (l, 0, 0)),
            pl.BlockSpec((None,) + wuq.shape[1:], lambda i: (l, 0, 0)),
            pl.BlockSpec((None,) + wukv.shape[1:], lambda i: (l, 0, 0)),
            pl.BlockSpec((TM, pe_w), rope_block),
            pl.BlockSpec((TM, pe_w), rope_block),
        ],
        out_specs=[
            pl.BlockSpec((MLA_HEADS, TM, QK_DIM), lambda i: (0, i, 0)),
            pl.BlockSpec((MLA_HEADS, TM, QK_DIM), lambda i: (0, i, 0)),
            pl.BlockSpec((MLA_HEADS, TM, v1_w), lambda i: (0, i, 0)),
        ],
        out_shape=[
            jax.ShapeDtypeStruct((MLA_HEADS, nt, QK_DIM), BF16),
            jax.ShapeDtypeStruct((MLA_HEADS, nt, QK_DIM), BF16),
            jax.ShapeDtypeStruct((MLA_HEADS, nt, v1_w), BF16),
        ],
        compiler_params=_cparams("parallel"),
        name="mla_qkv",
    )(p_mla, q_norm[:, None, :], kv_norm[:, None, :], wuq, wukv, cos_t, sin_t)


KV_CHUNK = 512
ATT_TQ = 1024


def _attn_kernel(q_ref, *refs):
    o_ref = refs[-1]
    q = q_ref[...]
    chunks = []
    for k_ref, v_ref in zip(refs[0:-1:2], refs[1:-1:2]):
        n = k_ref.shape[0]
        step = min(KV_CHUNK, n)
        chunks += [(k_ref, v_ref, c0, step) for c0 in range(0, n, step)]
    scores = lambda c: lax.dot_general(q, c[0][c[2]:c[2] + c[3], :], NT_DIMS, preferred_element_type=F32)
    m = jnp.full((q.shape[0], 1), -jnp.inf, F32)
    acc = jnp.zeros((q.shape[0], V_HEAD + LANES), F32)
    s_next = scores(chunks[0])
    for idx, c in enumerate(chunks):
        s = s_next
        if idx + 1 < len(chunks):
            s_next = scores(chunks[idx + 1])
        m_new = jnp.maximum(m, jnp.max(s, axis=-1, keepdims=True))
        p = jnp.exp2(s - m_new).astype(BF16)
        acc = jnp.exp2(m - m_new) * acc + jnp.dot(p, c[1][c[2]:c[2] + c[3], :], preferred_element_type=F32)
        m = m_new
    o_ref[...] = (acc[:, :V_HEAD] / acc[:, V_HEAD:]).astype(o_ref.dtype)


def _attention(q, k, v1, batch, seq, nctx):
    tq = ATT_TQ
    nq = seq // tq
    ctx0 = batch * seq // nctx
    v1_w = v1.shape[-1]
    a_lat = pl.pallas_call(
        _attn_kernel,
        grid=(batch, MLA_HEADS, nq),
        in_specs=[
            pl.BlockSpec((None, tq, QK_DIM), lambda b, h, i: (h, b * nq + i, 0)),
            pl.BlockSpec((None, nctx, QK_DIM), lambda b, h, i: (h, ctx0 + b, 0)),
            pl.BlockSpec((None, nctx, v1_w), lambda b, h, i: (h, ctx0 + b, 0)),
            pl.BlockSpec((None, seq, QK_DIM), lambda b, h, i: (h, b, 0)),
            pl.BlockSpec((None, seq, v1_w), lambda b, h, i: (h, b, 0)),
        ],
        out_specs=pl.BlockSpec((tq, V_HEAD), lambda b, h, i: (b * nq + i, h)),
        out_shape=jax.ShapeDtypeStruct((batch * seq, MLA_W), BF16),
        compiler_params=_cparams("parallel", "parallel", "arbitrary"),
        name="attn_lat",
    )(q, k, v1, k, v1)
    a_ctx = pl.pallas_call(
        _attn_kernel,
        grid=(batch, MLA_HEADS),
        in_specs=[
            pl.BlockSpec((None, nctx, QK_DIM), lambda b, h: (h, ctx0 + b, 0)),
            pl.BlockSpec((None, nctx, QK_DIM), lambda b, h: (h, ctx0 + b, 0)),
            pl.BlockSpec((None, nctx, v1_w), lambda b, h: (h, ctx0 + b, 0)),
        ],
        out_specs=pl.BlockSpec((nctx, V_HEAD), lambda b, h: (b, h)),
        out_shape=jax.ShapeDtypeStruct((batch * nctx, MLA_W), BF16),
        compiler_params=_cparams("parallel", "parallel"),
        name="attn_ctx",
    )(q, k, v1)
    return a_lat, a_ctx


def _scan_blocks(batch, seq, nctx):
    ncb = nctx // TB
    nlb = seq // TB
    ctx0 = batch * nlb

    def fwd(b, j):
        return jnp.where(j < ncb, ctx0 + b * ncb + j, b * nlb + (j - ncb))

    def bwd(b, j):
        return jnp.where(j < ncb, ctx0 + b * ncb + (ncb - 1 - j), b * nlb + (nlb - 1 - (j - ncb)))

    return ncb + nlb, fwd, bwd


def _dot(a, b, dims=NN_DIMS):
    return lax.dot_general(a.astype(BF16), b.astype(BF16), dims, preferred_element_type=F32)


def _scan_units(nheads):
    return [(d, h) for d in range(2) for h in range(nheads)]


def _mlstm_kernel(xf_ref, xb_ref, ktf_ref, ktb_ref, gcf_ref, gcb_ref, grf_ref, grb_ref, hf_ref, hb_ref, c_ref, m_ref):
    @pl.when(pl.program_id(1) == 0)
    def _():
        c_ref[...] = jnp.zeros_like(c_ref)
        m_ref[...] = jnp.full(m_ref.shape, M_INIT, F32)

    nch = TB // CHUNK
    row = lax.broadcasted_iota(jnp.int32, (CHUNK, CHUNK), 0)
    col = lax.broadcasted_iota(jnp.int32, (CHUNK, CHUNK), 1)
    kscale = ML_DK ** -0.5
    qw = ML_HEADS * ML_DK
    refs = ((xf_ref, ktf_ref, gcf_ref, grf_ref, hf_ref), (xb_ref, ktb_ref, gcb_ref, grb_ref, hb_ref))
    units = _scan_units(ML_HEADS)

    ones = jnp.ones((CHUNK, LANES), BF16)
    sel_row = lax.broadcasted_iota(jnp.int32, (P_GATE, 2 * ML_HEADS * LANES), 0)
    sel_blk = lax.broadcasted_iota(jnp.int32, (P_GATE, 2 * ML_HEADS * LANES), 1) // LANES
    loc = {}
    for ci in range(nch):
        for d in range(2):
            x_ref, kt_ref, gc_ref, gr_ref, _ = refs[d]
            ch = ci if d == 0 else nch - 1 - ci
            rs = slice(ch * CHUNK, (ch + 1) * CHUNK)
            src_col = jnp.where(sel_blk < ML_HEADS, d * 8 + ML_HEADS + sel_blk, d * 8 + sel_blk - ML_HEADS)
            select = jnp.where(sel_row == src_col, 1.0, 0.0).astype(BF16)
            g = gc_ref[rs, :]
            g_hi = g.astype(BF16)
            g_lo = (g - g_hi.astype(F32)).astype(BF16)
            rep = (jnp.dot(g_hi, select, preferred_element_type=F32)
                   + jnp.dot(g_lo, select, preferred_element_type=F32))
            mask = (col <= row) if d == 0 else (col >= row)
            last = CHUNK - 1 if d == 0 else 0
            for h in range(ML_HEADS):
                b_rep = rep[:, h * LANES:(h + 1) * LANES]
                pm_rep = rep[:, (ML_HEADS + h) * LANES:(ML_HEADS + h + 1) * LANES]
                d_row = gr_ref[d * 8 + h:d * 8 + h + 1, rs]
                b_end = b_rep[last:last + 1, :]
                loc[ci, d, h] = dict(
                    rs=rs, b_rep=b_rep, b_end=b_end,
                    log_d=jnp.where(mask, b_rep[:, :CHUNK] + d_row, -jnp.inf),
                    log_w=b_end[:, :CHUNK] + d_row,
                    q=x_ref[rs, h * ML_DK:(h + 1) * ML_DK],
                    kt=kt_ref[h * ML_DK:(h + 1) * ML_DK, rs],
                    v1=jnp.concatenate([x_ref[rs, 2 * qw + h * ML_DV:2 * qw + (h + 1) * ML_DV], ones], axis=-1),
                    row_max=b_rep + pm_rep,
                    w_max=b_end + pm_rep[last:last + 1, :])
    for u in loc.values():
        u["qk"] = _dot(u["q"], u["kt"])

    state = {(d, h): (c_ref[d * ML_HEADS + h], m_ref[d * ML_HEADS + h]) for d, h in units}
    for ci in range(nch):
        cur = [(dh, loc[(ci,) + dh], state[dh]) for dh in units]
        for _, u, (cn, m) in cur:
            log_inter = u["b_rep"] + m
            m_t = jnp.maximum(log_inter, u["row_max"])
            u["w_inter"] = jnp.exp(log_inter - m_t)
            u["floor"] = jnp.exp(-m_t)
            u["s"] = u["qk"] * jnp.exp(u["log_d"] - m_t[:, :CHUNK]) * kscale
        for _, u, (cn, m) in cur:
            u["qc"] = _dot(u["q"], cn)
            u["sv"] = _dot(u["s"], u["v1"])
        for (d, h), u, (cn, m) in cur:
            num = u["w_inter"] * u["qc"][:, :ML_DV] + u["sv"][:, :ML_DV]
            den = u["w_inter"] * u["qc"][:, ML_DV:] + u["sv"][:, ML_DV:]
            refs[d][4][u["rs"], h * ML_DV:(h + 1) * ML_DV] = num / jnp.maximum(jnp.abs(den), u["floor"])
            m_new = jnp.maximum(u["b_end"] + m, u["w_max"])
            u["decay"] = jnp.exp(u["b_end"] + m - m_new)
            u["m_new"] = m_new
            u["kwt"] = u["kt"].astype(F32) * (jnp.exp(u["log_w"] - m_new[:, :CHUNK]) * kscale)
        for _, u, _st in cur:
            u["upd"] = _dot(u["kwt"], u["v1"])
        for dh, u, (cn, m) in cur:
            state[dh] = (jnp.concatenate([u["decay"], u["decay"]], axis=-1) * cn + u["upd"], u["m_new"])
    for d, h in units:
        st = d * ML_HEADS + h
        c_ref[st], m_ref[st] = state[(d, h)]


def _mlstm(p_ml, kt, gcol, grow, batch, seq, nctx):
    nt = p_ml.shape[0]
    nsteps, fwd, bwd = _scan_blocks(batch, seq, nctx)
    xw = 2 * ML_HEADS * ML_DK + ML_W
    nst = 2 * ML_HEADS
    return pl.pallas_call(
        _mlstm_kernel,
        grid=(batch, nsteps),
        in_specs=[
            pl.BlockSpec((TB, xw), lambda b, j: (fwd(b, j), 0)),
            pl.BlockSpec((TB, xw), lambda b, j: (bwd(b, j), 0)),
            pl.BlockSpec((ML_HEADS * ML_DK, TB), lambda b, j: (0, fwd(b, j))),
            pl.BlockSpec((ML_HEADS * ML_DK, TB), lambda b, j: (0, bwd(b, j))),
            pl.BlockSpec((TB, P_GATE), lambda b, j: (fwd(b, j), 0)),
            pl.BlockSpec((TB, P_GATE), lambda b, j: (bwd(b, j), 0)),
            pl.BlockSpec((P_GATE, TB), lambda b, j: (0, fwd(b, j))),
            pl.BlockSpec((P_GATE, TB), lambda b, j: (0, bwd(b, j))),
        ],
        out_specs=[
            pl.BlockSpec((TB, ML_W), lambda b, j: (fwd(b, j), 0)),
            pl.BlockSpec((TB, ML_W), lambda b, j: (bwd(b, j), 0)),
        ],
        out_shape=[jax.ShapeDtypeStruct((nt, ML_W), F32)] * 2,
        scratch_shapes=[
            pltpu.VMEM((nst, ML_DK, ML_DV + LANES), F32),
            pltpu.VMEM((nst, 1, LANES), F32),
        ],
        compiler_params=_cparams("parallel", "arbitrary"),
        name="mlstm",
    )(p_ml, p_ml, kt, kt, gcol, gcol, grow, grow)


INV_BLOCK = 16


def _unit_triangular_inverses(mats):
    n = mats[0].shape[0]
    row = lax.broadcasted_iota(jnp.int32, (n, n), 0)
    col = lax.broadcasted_iota(jnp.int32, (n, n), 1)
    eye = jnp.where(row == col, 1.0, 0.0)
    same_block = row // INV_BLOCK == col // INV_BLOCK
    ps = [jnp.where(same_block, a, 0.0) for a in mats]
    xs = [eye - p for p in ps]
    for _ in range((INV_BLOCK - 1).bit_length() - 1):
        ps = [_dot(p, p) for p in ps]
        xs = [x + xp for x, xp in zip(xs, [_dot(x, p) for x, p in zip(xs, ps)])]
    size = INV_BLOCK
    while size < n:
        sel = (row // (2 * size) == col // (2 * size)) & (row // size != col // size)
        ts = [_dot(x, jnp.where(sel, a, 0.0)) for x, a in zip(xs, mats)]
        xs = [x - tx for x, tx in zip(xs, [_dot(t, x) for t, x in zip(ts, xs)])]
        size *= 2
    return xs


def _gdn_kernel(xf_ref, xb_ref, gcf_ref, gcb_ref, grf_ref, grb_ref, of_ref, ob_ref, s_ref):
    @pl.when(pl.program_id(1) == 0)
    def _():
        s_ref[...] = jnp.zeros_like(s_ref)

    nch = TB // CHUNK
    row = lax.broadcasted_iota(jnp.int32, (CHUNK, CHUNK), 0)
    col = lax.broadcasted_iota(jnp.int32, (CHUNK, CHUNK), 1)
    nk = GD_HEADS * GD_DK
    refs = ((xf_ref, gcf_ref, grf_ref, of_ref), (xb_ref, gcb_ref, grb_ref, ob_ref))
    units = _scan_units(GD_HEADS)

    loc = {}
    for ci in range(nch):
        for d, h in units:
            x_ref, gc_ref, gr_ref, _ = refs[d]
            ch = ci if d == 0 else nch - 1 - ci
            rs = slice(ch * CHUNK, (ch + 1) * CHUNK)
            cb = 16 + d * 8 + h
            cg = 16 + d * 8 + 4 + h
            beta = jnp.broadcast_to(gc_ref[rs, cb:cb + 1], (CHUNK, LANES))
            g_col = jnp.broadcast_to(gc_ref[rs, cg:cg + 1], (CHUNK, LANES))
            g_row = gr_ref[cg:cg + 1, rs]
            g_end = g_col[CHUNK - 1:CHUNK, :] if d == 0 else g_col[0:1, :]
            incl = (col <= row) if d == 0 else (col >= row)
            q = x_ref[rs, h * GD_DK:(h + 1) * GD_DK]
            k = x_ref[rs, nk + h * GD_DK:nk + (h + 1) * GD_DK]
            v = x_ref[rs, 2 * nk + h * GD_DV:2 * nk + (h + 1) * GD_DV]
            kb = k * beta
            e_col = jnp.exp(g_col)
            loc[ci, d, h] = dict(
                rs=rs, k=k, kb=kb, q=q,
                strict=(col < row) if d == 0 else (col > row),
                decay=jnp.exp(jnp.where(incl, g_col[:, :CHUNK] - g_row, -jnp.inf)),
                rhs=jnp.concatenate([v * beta, kb * e_col], axis=-1),
                qg=q * e_col, kd=k * jnp.exp(g_end - g_col), e_end=jnp.exp(g_end))
    us = list(loc.values())
    for u in us:
        kk_qk = _dot(jnp.concatenate([u["kb"], u["q"]], axis=0), u["k"], NT_DIMS)
        u["kk"] = kk_qk[:CHUNK]
        u["attn"] = kk_qk[CHUNK:] * u["decay"]
    tinvs = _unit_triangular_inverses([jnp.where(u["strict"], u["kk"] * u["decay"], 0.0) for u in us])
    for u, tinv in zip(us, tinvs):
        u["uw"] = _dot(tinv, u["rhs"])

    state = {(d, h): s_ref[d * GD_HEADS + h] for d, h in units}
    for ci in range(nch):
        cur = [(dh, loc[(ci,) + dh]) for dh in units]
        for dh, u in cur:
            u["ws_qs"] = _dot(jnp.concatenate([u["uw"][:, GD_DV:], u["qg"]], axis=0), state[dh])
        for dh, u in cur:
            u["v_new"] = u["uw"][:, :GD_DV] - u["ws_qs"][:CHUNK]
            u["qs"] = u["ws_qs"][CHUNK:]
        for dh, u in cur:
            u["av"] = _dot(u["attn"], u["v_new"])
            u["upd"] = _dot(u["kd"], u["v_new"], TN_DIMS)
        for (d, h), u in cur:
            refs[d][3][u["rs"], h * GD_DV:(h + 1) * GD_DV] = u["qs"] + u["av"]
            state[(d, h)] = state[(d, h)] * u["e_end"] + u["upd"]
    for d, h in units:
        s_ref[d * GD_HEADS + h] = state[(d, h)]


def _gdn(qkv, gcol, grow, batch, seq, nctx):
    nt = qkv.shape[0]
    nsteps, fwd, bwd = _scan_blocks(batch, seq, nctx)
    cw = qkv.shape[1]
    return pl.pallas_call(
        _gdn_kernel,
        grid=(batch, nsteps),
        in_specs=[
            pl.BlockSpec((TB, cw), lambda b, j: (fwd(b, j), 0)),
            pl.BlockSpec((TB, cw), lambda b, j: (bwd(b, j), 0)),
            pl.BlockSpec((TB, P_GATE), lambda b, j: (fwd(b, j), 0)),
            pl.BlockSpec((TB, P_GATE), lambda b, j: (bwd(b, j), 0)),
            pl.BlockSpec((P_GATE, TB), lambda b, j: (0, fwd(b, j))),
            pl.BlockSpec((P_GATE, TB), lambda b, j: (0, bwd(b, j))),
        ],
        out_specs=[
            pl.BlockSpec((TB, GD_W), lambda b, j: (fwd(b, j), 0)),
            pl.BlockSpec((TB, GD_W), lambda b, j: (bwd(b, j), 0)),
        ],
        out_shape=[jax.ShapeDtypeStruct((nt, GD_W), F32)] * 2,
        scratch_shapes=[pltpu.VMEM((2 * GD_HEADS, GD_DK, GD_DV), F32)],
        compiler_params=_cparams("parallel", "arbitrary"),
        name="gdn",
    )(qkv, qkv, gcol, gcol, grow, grow)


def _head_rms(y, gain, nheads, width):
    parts = []
    for h in range(nheads):
        seg = y[:, h * width:(h + 1) * width]
        parts.append(_rms(seg))
    return jnp.concatenate(parts, axis=-1) * gain


def _merge_kernel(*refs, n_x, lat_tiles):
    x_refs = refs[:n_x]
    (mod_ref, al_ref, ac_ref, hf_ref, hb_ref, o_ref, gf_ref, gb_ref, z_ref,
     an_ref, mn_ref, gn_ref, w_ref, out_ref) = refs[n_x:]
    a = _token_tile((al_ref, ac_ref), lat_tiles).astype(F32)
    ya = _head_rms(a, an_ref[...], MLA_HEADS, V_HEAD)
    ym = _head_rms(hf_ref[...] + hb_ref[...], mn_ref[...], ML_HEADS, ML_DV) * _sigmoid(o_ref[...].astype(F32))
    z = z_ref[...].astype(F32)
    yg = _head_rms(gf_ref[...] + gb_ref[...], gn_ref[...], GD_HEADS, GD_DV) * (z * _sigmoid(z))
    y = jnp.dot(ya.astype(BF16), w_ref[0:MLA_W, :], preferred_element_type=F32)
    y = y + jnp.dot(ym.astype(BF16), w_ref[MLA_W:MLA_W + ML_W, :], preferred_element_type=F32)
    y = y + jnp.dot(yg.astype(BF16), w_ref[MLA_W + ML_W:, :], preferred_element_type=F32)
    out_ref[...] = _token_tile(x_refs, lat_tiles) + mod_ref[2:3, :] * y


def _merge(l, xs, mod, a_lat, a_ctx, hf, hb, p_ml, gf, gb, gd_z, mla_norm, ml_norm, gd_norm, w_out, seq, batch, nrows):
    d = xs[0].shape[1]
    lat_tiles = a_lat.shape[0] // TM
    assert a_ctx.shape[0] == TM
    row = functools.partial(_mod_row, seq=seq, batch=batch)
    tile = lambda w, cb=0: pl.BlockSpec((TM, w), lambda i: (i, cb))
    vec = lambda w: pl.BlockSpec((None, 1, w), lambda i: (l, 0, 0))
    return pl.pallas_call(
        functools.partial(_merge_kernel, n_x=len(xs), lat_tiles=lat_tiles),
        grid=(nrows // TM,),
        in_specs=_token_specs(xs, d) + [
            pl.BlockSpec((None, None, 6, d), lambda i: (l, row(i), 0, 0)),
            *_token_specs((a_lat, a_ctx), MLA_W),
            tile(ML_W), tile(ML_W), tile(ML_W, 2),
            tile(GD_W), tile(GD_W), tile(GD_W),
            vec(MLA_W), vec(ML_W), vec(GD_W),
            pl.BlockSpec((None, MIX_W, d), lambda i: (l, 0, 0)),
        ],
        out_specs=tile(d),
        out_shape=jax.ShapeDtypeStruct((nrows, d), F32),
        compiler_params=_cparams("parallel"),
        name="merge",
    )(*xs, mod, a_lat, a_ctx, hf, hb, p_ml, gf, gb, gd_z, mla_norm[:, None, :], ml_norm[:, None, :],
      jnp.tile(gd_norm, (1, GD_HEADS))[:, None, :], w_out)


def _mlp_kernel(x_ref, mod_ref, nw_ref, w1_ref, w2_ref, *rest):
    out_ref, h_ref, acc_ref = rest[-3:]
    j = pl.program_id(1)

    @pl.when(j == 0)
    def _():
        h_ref[...] = _norm_mod(x_ref[...], nw_ref[...], mod_ref[3:4, :], mod_ref[4:5, :]).astype(BF16)
        acc_ref[...] = jnp.zeros_like(acc_ref)

    a = jnp.maximum(jnp.dot(h_ref[...], w1_ref[...], preferred_element_type=F32), 0.0)
    acc_ref[...] += jnp.dot((a * a).astype(BF16), w2_ref[...], preferred_element_type=F32)

    @pl.when(j == pl.num_programs(1) - 1)
    def _():
        y = x_ref[...] + mod_ref[5:6, :] * acc_ref[...]
        out_ref[...] = _rms(y) * rest[0][...] if len(rest) == 4 else y


def _mlp(l, xg, mod, norm_w, w1, w2, seq, batch, nrows, final_w=None):
    d = xg.shape[1]
    f = w1.shape[-1]
    tf = min(MLP_TF, f)
    row = functools.partial(_mod_row, seq=seq, batch=batch)
    extra = [] if final_w is None else [final_w.reshape(1, d)]
    return pl.pallas_call(
        _mlp_kernel,
        grid=(nrows // TM, f // tf),
        in_specs=[
            pl.BlockSpec((TM, d), lambda i, j: (i, 0)),
            pl.BlockSpec((None, None, 6, d), lambda i, j: (l, row(i), 0, 0)),
            pl.BlockSpec((None, 1, d), lambda i, j: (l, 0, 0)),
            pl.BlockSpec((None, d, tf), lambda i, j: (l, 0, j)),
            pl.BlockSpec((None, tf, d), lambda i, j: (l, j, 0)),
        ] + [pl.BlockSpec((1, d), lambda i, j: (0, 0)) for _ in extra],
        out_specs=pl.BlockSpec((TM, d), lambda i, j: (i, 0)),
        out_shape=jax.ShapeDtypeStruct((nrows, d), F32),
        scratch_shapes=[pltpu.VMEM((TM, d), BF16), pltpu.VMEM((TM, d), F32)],
        compiler_params=_cparams("parallel", "arbitrary"),
        name="mlp",
    )(xg, mod, norm_w[:, None, :], w1, w2, *extra)


def _regroup_w_in_kernel(w_ref, o_ref):
    o_ml = Q_LORA + KV_LORA + QK_ROPE
    o_mlg = o_ml + P_ML
    o_gd = o_mlg + 4 * ML_HEADS
    o_gdg = o_gd + P_GD
    rows = w_ref.shape[0]
    zeros = lambda n: jnp.zeros((rows, n), BF16)
    piece = lambda a, b: w_ref[:, a:b].astype(BF16)
    o_ref[...] = jnp.concatenate([
        piece(0, o_ml), zeros(P_MLA - o_ml),
        piece(o_ml, o_mlg),
        piece(o_gd, o_gdg),
        piece(o_mlg, o_gd), piece(o_gdg, o_gdg + 4 * GD_HEADS), zeros(P_GATE - 4 * ML_HEADS - 4 * GD_HEADS),
    ], axis=-1)


def _regroup_w_in(w_in):
    nl, d, n_in = w_in.shape
    tr = 256
    return pl.pallas_call(
        _regroup_w_in_kernel,
        grid=(nl, d // tr),
        in_specs=[pl.BlockSpec((None, tr, n_in), lambda l, i: (l, i, 0))],
        out_specs=pl.BlockSpec((None, tr, P_ALL), lambda l, i: (l, i, 0)),
        out_shape=jax.ShapeDtypeStruct((nl, d, P_ALL), BF16),
        compiler_params=_cparams("parallel", "parallel"),
        name="regroup_w_in",
    )(w_in)


def _regroup_heads(w, first):
    nl, kdim, _ = w.shape
    w4 = w.reshape(nl, kdim, MLA_HEADS, -1)
    return jnp.concatenate([w4[..., :first].reshape(nl, kdim, -1), w4[..., first:].reshape(nl, kdim, -1)],
                           axis=-1).astype(BF16)


def _rope_tables(seq):
    half = QK_ROPE // 2
    t = jnp.arange(seq, dtype=jnp.int32)
    inv = ROPE_BASE ** (-jnp.arange(0, half, 2, dtype=F32) / half)
    ang_r = (t // GRID_W).astype(F32)[:, None] * inv
    ang_c = (t % GRID_W).astype(F32)[:, None] * inv
    cos = jnp.concatenate([jnp.cos(ang_r)] * 2 + [jnp.cos(ang_c)] * 2, axis=-1)
    sin = jnp.concatenate([-jnp.sin(ang_r), jnp.sin(ang_r), -jnp.sin(ang_c), jnp.sin(ang_c)], axis=-1)
    cos = jnp.concatenate([cos, jnp.ones((TM, QK_ROPE), F32)], axis=0)
    sin = jnp.concatenate([sin, jnp.zeros((TM, QK_ROPE), F32)], axis=0)
    return jnp.tile(cos, (1, MLA_HEADS)), jnp.tile(sin, (1, MLA_HEADS))


def _gate_params(ml_gate_bias, gd_a_log, gd_dt_bias):
    nl = ml_gate_bias.shape[0]
    zh = jnp.zeros((nl, GD_HEADS), F32)
    bias = jnp.concatenate([ml_gate_bias, zh, gd_dt_bias[:, 0], zh, gd_dt_bias[:, 1]], axis=-1)
    alog = jnp.concatenate([jnp.zeros((nl, 4 * ML_HEADS), F32), zh, gd_a_log[:, 0], zh, gd_a_log[:, 1]], axis=-1)
    par = jnp.stack([bias, alog], axis=1)
    return jnp.pad(par, ((0, 0), (0, 6), (0, P_GATE - par.shape[-1])))


def kernel(x, c, ctx, c_ctx, w_ada, b_ada, norm1, norm2, w_in, mla_q_norm, mla_w_uq, mla_kv_norm, mla_w_ukv, mla_out_norm, ml_gate_bias, ml_out_norm, gd_conv, gd_a_log, gd_dt_bias, gd_out_norm, w_out, w_mlp1, w_mlp2, final_norm):
    batch, seq, d = x.shape
    nctx = ctx.shape[1]
    depth = w_ada.shape[0]
    n_lat = batch * seq
    n_all = n_lat + batch * nctx
    assert seq % TM == 0 and (batch * nctx) % TM == 0 and nctx % TB == 0 and TM % nctx == 0

    xs = (x.reshape(n_lat, d), ctx.reshape(batch * nctx, d))
    mod_all = _ada(jnp.concatenate([c, c_ctx[None]], axis=0), w_ada, b_ada).reshape(depth, batch + 1, 6, d)
    w_in_g = _regroup_w_in(w_in)
    wuq_g = _regroup_heads(mla_w_uq, QK_NOPE)
    wukv_g = _regroup_heads(mla_w_ukv, QK_NOPE)
    w_out_b = w_out.astype(BF16)
    w1_b = w_mlp1.astype(BF16)
    w2_b = w_mlp2.astype(BF16)
    cos_t, sin_t = _rope_tables(seq)
    gate_par = _gate_params(ml_gate_bias, gd_a_log, gd_dt_bias)

    for l in range(depth):
        last = l == depth - 1
        p_mla, p_ml, ml_kt, gd_qkv, gd_z, gcol, grow = _inproj(l, xs, mod_all, norm1, w_in_g, gate_par, gd_conv,
                                                               seq, batch, nctx)
        q, k, v1 = _mla_qkv_call(l, p_mla, mla_q_norm, mla_kv_norm, wuq_g, wukv_g, cos_t, sin_t, seq, n_lat)
        a_lat, a_ctx = _attention(q, k, v1, batch, seq, nctx)
        hf, hb = _mlstm(p_ml, ml_kt, gcol, grow, batch, seq, nctx)
        gf, gb = _gdn(gd_qkv, gcol, grow, batch, seq, nctx)
        nrows = n_lat if last else n_all
        xg = _merge(l, xs, mod_all, a_lat, a_ctx, hf, hb, p_ml, gf, gb, gd_z, mla_out_norm, ml_out_norm,
                    gd_out_norm, w_out_b, seq, batch, nrows)
        xs = (_mlp(l, xg, mod_all, norm2, w1_b, w2_b, seq, batch, nrows, final_norm if last else None),)
    return xs[0].reshape(batch, seq, d)
```

```python
import functools

import jax
import jax.numpy as jnp
from jax import lax
from jax.experimental import pallas as pl
from jax.experimental.pallas import tpu as pltpu

F32 = jnp.float32
BF16 = jnp.bfloat16

EPS = 1e-6
LOG2_E = 1.4426950408889634
M_INIT = -1e30
GRID_W = 64
ROPE_BASE = 10000.0

MLA_HEADS = 8
Q_LORA = 512
KV_LORA = 256
QK_NOPE = 128
QK_ROPE = 64
V_HEAD = 128
QK_DIM = QK_NOPE + QK_ROPE
ML_HEADS = 4
ML_DK = 64
ML_DV = 128
GD_HEADS = 4
GD_DK = 128
GD_DV = 128
CHUNK = 64
CONV_W = 5

MLA_W = MLA_HEADS * V_HEAD
ML_W = ML_HEADS * ML_DV
GD_W = GD_HEADS * GD_DV
MIX_W = MLA_W + ML_W + GD_W

P_MLA = 896
P_ML = 1536
P_GD = 2048
P_GATE = 128
P_ALL = P_MLA + P_ML + P_GD + P_GATE

TM = 512
MLP_TF = 1024
ADA_COLS = 1024
TB = 256
HALO = 16
GATE_ROWS = 256
LANES = 128

VMEM_LIMIT_V7X = 56 * 1024 * 1024

NT_DIMS = (((1,), (1,)), ((), ()))
TN_DIMS = (((0,), (0,)), ((), ()))
NN_DIMS = (((1,), (0,)), ((), ()))


def _cparams(*sem):
    return pltpu.CompilerParams(dimension_semantics=sem, vmem_limit_bytes=VMEM_LIMIT_V7X)


def _sigmoid(x):
    return 1.0 / (1.0 + jnp.exp(-x))


def _rms(x):
    return x * lax.rsqrt(jnp.mean(x * x, axis=-1, keepdims=True) + EPS)


def _norm_mod(x, norm_w, shift, scale):
    return _rms(x) * norm_w * (1.0 + scale) + shift


def _mod_row(i, seq, batch):
    return jnp.minimum((i * TM) // seq, batch)


def _ada_kernel(cb_ref, w_ref, b_ref, o_ref, s_ref):
    @pl.when((pl.program_id(0) == 0) & (pl.program_id(1) == 0))
    def _():
        cb = cb_ref[...]
        s_ref[...] = cb * _sigmoid(cb)

    rows, d = cb_ref.shape[0], cb_ref.shape[1]
    tn = w_ref.shape[-1]
    tc = next(t for t in (ADA_COLS, ADA_COLS // 2, LANES) if tn % t == 0)
    sub = 8
    for c0 in range(0, tn, tc):
        def body(kb, accs):
            k0 = pl.multiple_of(kb * sub, sub)
            w = w_ref[pl.ds(k0, sub), c0:c0 + tc]
            return tuple(acc + w * jnp.tile(s_ref[r, pl.ds(k0, sub), :], (1, tc // LANES))
                         for r, acc in enumerate(accs))

        accs = lax.fori_loop(0, d // sub, body, tuple(jnp.zeros((sub, tc), F32) for _ in range(rows)), unroll=4)
        for r in range(rows):
            o_ref[r:r + 1, c0:c0 + tc] = jnp.sum(accs[r], axis=0, keepdims=True) + b_ref[:, c0:c0 + tc]


def _ada(cvec, w_ada, b_ada):
    nl, d, n6 = w_ada.shape
    r = cvec.shape[0]
    tn = next(t for t in (2048, 1536, 1024, 512, LANES) if n6 % t == 0)
    cb = jnp.broadcast_to(cvec[:, :, None], (r, d, LANES))
    return pl.pallas_call(
        _ada_kernel,
        grid=(nl, n6 // tn),
        in_specs=[
            pl.BlockSpec((r, d, LANES), lambda l, j: (0, 0, 0)),
            pl.BlockSpec((None, d, tn), lambda l, j: (l, 0, j)),
            pl.BlockSpec((None, 1, tn), lambda l, j: (l, 0, j)),
        ],
        out_specs=pl.BlockSpec((None, r, tn), lambda l, j: (l, 0, j)),
        out_shape=jax.ShapeDtypeStruct((nl, r, n6), F32),
        scratch_shapes=[pltpu.VMEM((r, d, LANES), F32)],
        compiler_params=_cparams("arbitrary", "arbitrary"),
        name="ada",
    )(cb, w_ada, b_ada.reshape(nl, 1, n6))


def _token_specs(xs, width):
    if len(xs) == 1:
        return [pl.BlockSpec((TM, width), lambda i: (i, 0))]
    nl = xs[0].shape[0] // TM
    return [pl.BlockSpec((TM, width), lambda i: (jnp.minimum(i, nl - 1), 0)),
            pl.BlockSpec((TM, width), lambda i: (jnp.maximum(i - nl, 0), 0))]


def _token_tile(x_refs, lat_tiles):
    if len(x_refs) == 1:
        return x_refs[0][...]
    return jnp.where(pl.program_id(0) < lat_tiles, x_refs[0][...], x_refs[1][...])


def _gate_columns(g, bias, a_log):
    tb = g.shape[0]
    z = g + bias
    neg_a = -jnp.exp(a_log)
    lane = lax.broadcasted_iota(jnp.int32, z.shape, 1)
    kind = (lane // 4) % 4
    is_ml = lane < 16
    is_cum = (kind % 2 == 1) & (lane < 32)
    is_bwd = kind >= 2
    soft = jnp.log(1.0 + jnp.exp(-jnp.abs(z)))
    log_sig = jnp.minimum(z, 0.0) - soft
    softplus = jnp.maximum(z, 0.0) + soft
    pre = jnp.where(is_ml, jnp.where(is_cum, log_sig, z), jnp.where(is_cum, neg_a * softplus, _sigmoid(z)))
    r = lax.broadcasted_iota(jnp.int32, (tb, tb), 0)
    c = lax.broadcasted_iota(jnp.int32, (tb, tb), 1)
    same = (r // CHUNK) == (c // CHUNK)
    lower = jnp.where(same & (c <= r), 1.0, 0.0).astype(BF16)
    upper = jnp.where(same & (c >= r), 1.0, 0.0).astype(BF16)
    hi = pre.astype(BF16)
    rest = pre - hi.astype(F32)
    mid = rest.astype(BF16)
    parts = (hi, mid, (rest - mid.astype(F32)).astype(BF16))
    cum_f = sum(jnp.dot(lower, p, preferred_element_type=F32) for p in parts)
    cum_b = sum(jnp.dot(upper, p, preferred_element_type=F32) for p in parts)
    out = jnp.where(is_cum, jnp.where(is_bwd, cum_b, cum_f), pre)
    is_mli = is_ml & jnp.logical_not(is_cum)
    d = out - pltpu.roll(out, P_GATE - ML_HEADS, 1)
    t = lax.broadcasted_iota(jnp.int32, z.shape, 0) % CHUNK
    run = d
    shift = 1
    while shift < CHUNK:
        prev_f = jnp.where(t >= shift, pltpu.roll(run, shift, 0), -jnp.inf)
        prev_b = jnp.where(t < CHUNK - shift, pltpu.roll(run, tb - shift, 0), -jnp.inf)
        run = jnp.maximum(run, jnp.where(is_bwd, prev_b, prev_f))
        shift *= 2
    return jnp.where(is_mli, run, out), jnp.where(is_mli, d, out).T


def _short_conv_heads(buf_ref, cw_ref, tap_ok):
    half = CONV_W // 2
    y = None
    for j in range(CONV_W):
        term = buf_ref[HALO - half + j:HALO - half + j + TM, :] * cw_ref[j:j + 1, :]
        if tap_ok[j] is not None:
            term = term * tap_ok[j]
        y = term if y is None else y + term
    y = y * _sigmoid(y)
    nk = GD_HEADS * GD_DK
    parts = []
    for h in range(2 * GD_HEADS):
        seg = y[:, h * GD_DK:(h + 1) * GD_DK]
        seg = seg * lax.rsqrt(jnp.sum(seg * seg, axis=-1, keepdims=True) + EPS)
        parts.append(seg * (GD_DK ** -0.5) if h < GD_HEADS else seg)
    return jnp.concatenate(parts + [y[:, 2 * nk:]], axis=-1)


def _inproj_kernel(*refs, n_x, lat_tiles, seq_tiles, nctx):
    x_refs = refs[:n_x]
    (xp_ref, xn_ref, mod_ref, nw_ref, w_ref, gpar_ref, cw_ref,
     omla_ref, oml_ref, kt_ref, gdq_ref, z_ref, gcol_ref, grow_ref, buf_ref) = refs[n_x:]
    i = pl.program_id(0)
    in_lat = i < lat_tiles
    first = jnp.logical_or(jnp.logical_not(in_lat), i % seq_tiles == 0)
    last = jnp.logical_or(jnp.logical_not(in_lat), i % seq_tiles == seq_tiles - 1)

    x_ext = jnp.concatenate([xp_ref[...], _token_tile(x_refs, lat_tiles), xn_ref[...]], axis=0)
    h_ext = _norm_mod(x_ext, nw_ref[...], mod_ref[0:1, :], mod_ref[1:2, :]).astype(BF16)
    h = h_ext[HALO:HALO + TM]
    proj = lambda lhs, c0, n: jnp.dot(lhs, w_ref[:, c0:c0 + n], preferred_element_type=F32)

    c_gd = P_MLA + P_ML
    qkv_w = 2 * GD_HEADS * GD_DK + GD_W
    p_qkv = proj(h_ext, c_gd, qkv_w)
    buf_ref[0:HALO, :] = jnp.where(first, 0.0, p_qkv[0:HALO])
    buf_ref[HALO:HALO + TM, :] = p_qkv[HALO:HALO + TM]
    buf_ref[HALO + TM:, :] = jnp.where(last, 0.0, p_qkv[HALO + TM:])
    gates = proj(h, c_gd + P_GD, P_GATE)
    omla_ref[...] = proj(h, 0, P_MLA).astype(omla_ref.dtype)
    p_ml = proj(h, P_MLA, P_ML)
    oml_ref[...] = p_ml.astype(oml_ref.dtype)
    kw = ML_HEADS * ML_DK
    kt_ref[...] = p_ml[:, kw:2 * kw].T.astype(kt_ref.dtype)
    z_ref[...] = proj(h, c_gd + qkv_w, GD_W).astype(z_ref.dtype)

    pos = lax.broadcasted_iota(jnp.int32, (TM, LANES), 0) % nctx
    reps = qkv_w // LANES
    tap_ok = []
    for j in range(CONV_W):
        o = j - CONV_W // 2
        crosses = (pos + o < 0) | (pos + o >= nctx)
        keep = jnp.where(jnp.logical_or(in_lat, jnp.logical_not(crosses)), 1.0, 0.0)
        tap_ok.append(None if o == 0 or TM <= nctx else jnp.tile(keep, (1, reps)))
    gdq_ref[...] = _short_conv_heads(buf_ref, cw_ref, tap_ok).astype(gdq_ref.dtype)
    for r0 in range(0, TM, GATE_ROWS):
        col, row_t = _gate_columns(gates[r0:r0 + GATE_ROWS], gpar_ref[0:1, :], gpar_ref[1:2, :])
        gcol_ref[r0:r0 + GATE_ROWS, :] = col
        grow_ref[:, r0:r0 + GATE_ROWS] = row_t


def _inproj(l, xs, mod, norm_w, w, gate_par, conv_w, seq, batch, nctx):
    nt = sum(x.shape[0] for x in xs)
    d = xs[0].shape[1]
    row = functools.partial(_mod_row, seq=seq, batch=batch)
    lat_tiles = batch * seq // TM
    hpt = TM // HALO
    n_halo = xs[0].shape[0] // HALO
    qkv_w = 2 * GD_HEADS * GD_DK + GD_W
    kw = ML_HEADS * ML_DK
    tile = lambda n: pl.BlockSpec((TM, n), lambda i: (i, 0))
    outs = [
        (tile(P_MLA), (nt, P_MLA), BF16),
        (tile(P_ML), (nt, P_ML), BF16),
        (pl.BlockSpec((kw, TM), lambda i: (0, i)), (kw, nt), BF16),
        (tile(qkv_w), (nt, qkv_w), BF16),
        (tile(GD_W), (nt, GD_W), BF16),
        (tile(P_GATE), (nt, P_GATE), F32),
        (pl.BlockSpec((P_GATE, TM), lambda i: (0, i)), (P_GATE, nt), F32),
    ]
    kern = functools.partial(_inproj_kernel, n_x=len(xs), lat_tiles=lat_tiles, seq_tiles=seq // TM, nctx=nctx)
    return pl.pallas_call(
        kern,
        grid=(nt // TM,),
        in_specs=_token_specs(xs, d) + [
            pl.BlockSpec((HALO, d), lambda i: (jnp.clip(i * hpt - 1, 0, n_halo - 1), 0)),
            pl.BlockSpec((HALO, d), lambda i: (jnp.clip((i + 1) * hpt, 0, n_halo - 1), 0)),
            pl.BlockSpec((None, None, 6, d), lambda i: (l, row(i), 0, 0)),
            pl.BlockSpec((None, 1, d), lambda i: (l, 0, 0)),
            pl.BlockSpec((None, d, P_ALL), lambda i: (l, 0, 0), pipeline_mode=pl.Buffered(1)),
            pl.BlockSpec((None, 8, P_GATE), lambda i: (l, 0, 0)),
            pl.BlockSpec((None, CONV_W, qkv_w), lambda i: (l, 0, 0)),
        ],
        out_specs=[o[0] for o in outs],
        out_shape=[jax.ShapeDtypeStruct(o[1], o[2]) for o in outs],
        scratch_shapes=[pltpu.VMEM((TM + 2 * HALO, qkv_w), F32)],
        compiler_params=_cparams("parallel"),
        name="inproj",
    )(*xs, xs[0], xs[0], mod, norm_w[:, None, :], w, gate_par, conv_w)


def _swap16(x):
    n = x.shape[-1]
    lane = lax.broadcasted_iota(jnp.int32, x.shape, 1)
    return jnp.where(lane % 32 < 16, pltpu.roll(x, n - 16, 1), pltpu.roll(x, 16, 1))


def _mla_qkv_kernel(p_ref, qn_ref, kvn_ref, wuq_ref, wukv_ref, cos_ref, sin_ref, q_ref, k_ref, v_ref):
    p = p_ref[...].astype(F32)
    cq = p[:, :Q_LORA]
    ckv = p[:, Q_LORA:Q_LORA + KV_LORA]
    kpe = p[:, Q_LORA + KV_LORA:]
    cos = cos_ref[...]
    sin = sin_ref[...]
    qn = (_rms(cq) * qn_ref[...]).astype(BF16)
    q = jnp.dot(qn, wuq_ref[...], preferred_element_type=F32) * (QK_DIM ** -0.5 * LOG2_E)
    nope_w = MLA_HEADS * QK_NOPE
    qpe = q[:, nope_w:]
    qpe = qpe * cos + _swap16(qpe) * sin
    kvn = (_rms(ckv) * kvn_ref[...]).astype(BF16)
    kv = jnp.dot(kvn, wukv_ref[...], preferred_element_type=F32)
    kpe = kpe * cos[:, :LANES] + _swap16(kpe) * sin[:, :LANES]
    kpe = kpe[:, :QK_ROPE]
    ones = jnp.ones((q.shape[0], LANES), BF16)
    for h in range(MLA_HEADS):
        qh = jnp.concatenate([q[:, h * QK_NOPE:(h + 1) * QK_NOPE], qpe[:, h * QK_ROPE:(h + 1) * QK_ROPE]], axis=-1)
        kh = jnp.concatenate([kv[:, h * QK_NOPE:(h + 1) * QK_NOPE], kpe], axis=-1)
        q_ref[h] = qh.astype(BF16)
        k_ref[h] = kh.astype(BF16)
        v_ref[h, :, :V_HEAD] = kv[:, nope_w + h * V_HEAD:nope_w + (h + 1) * V_HEAD].astype(BF16)
        v_ref[h, :, V_HEAD:] = ones


def _mla_qkv_call(l, p_mla, q_norm, kv_norm, wuq, wukv, cos_t, sin_t, seq, n_lat_rows):
    nt = p_mla.shape[0]
    lat_tiles = seq // TM
    nl_tiles = n_lat_rows // TM
    pe_w = MLA_HEADS * QK_ROPE
    v1_w = V_HEAD + LANES

    def rope_block(i):
        return (jnp.where(i < nl_tiles, i % lat_tiles, lat_tiles), 0)

    return pl.pallas_call(
        _mla_qkv_kernel,
        grid=(nt // TM,),
        in_specs=[
            pl.BlockSpec((TM, P_MLA), lambda i: (i, 0)),
            pl.BlockSpec((None, 1, Q_LORA), lambda i: (l, 0, 0)),
            pl.BlockSpec((None, 1, KV_LORA), lambda i: (l, 0, 0)),
            pl.BlockSpec((None,) + wuq.shape[1:], lambda i: (l, 0, 0)),
            pl.BlockSpec((None,) + wukv.shape[1:], lambda i: (l, 0, 0)),
            pl.BlockSpec((TM, pe_w), rope_block),
            pl.BlockSpec((TM, pe_w), rope_block),
        ],
        out_specs=[
            pl.BlockSpec((MLA_HEADS, TM, QK_DIM), lambda i: (0, i, 0)),
            pl.BlockSpec((MLA_HEADS, TM, QK_DIM), lambda i: (0, i, 0)),
            pl.BlockSpec((MLA_HEADS, TM, v1_w), lambda i: (0, i, 0)),
        ],
        out_shape=[
            jax.ShapeDtypeStruct((MLA_HEADS, nt, QK_DIM), BF16),
            jax.ShapeDtypeStruct((MLA_HEADS, nt, QK_DIM), BF16),
            jax.ShapeDtypeStruct((MLA_HEADS, nt, v1_w), BF16),
        ],
        compiler_params=_cparams("parallel"),
        name="mla_qkv",
    )(p_mla, q_norm[:, None, :], kv_norm[:, None, :], wuq, wukv, cos_t, sin_t)


KV_CHUNK = 512
ATT_TQ = 1024


def _attn_kernel(q_ref, *refs):
    o_ref = refs[-1]
    q = q_ref[...]
    chunks = []
    for k_ref, v_ref in zip(refs[0:-1:2], refs[1:-1:2]):
        n = k_ref.shape[0]
        step = min(KV_CHUNK, n)
        chunks += [(k_ref, v_ref, c0, step) for c0 in range(0, n, step)]
    scores = lambda c: lax.dot_general(q, c[0][c[2]:c[2] + c[3], :], NT_DIMS, preferred_element_type=F32)
    m = jnp.full((q.shape[0], 1), -jnp.inf, F32)
    acc = jnp.zeros((q.shape[0], V_HEAD + LANES), F32)
    s_next = scores(chunks[0])
    for idx, c in enumerate(chunks):
        s = s_next
        if idx + 1 < len(chunks):
            s_next = scores(chunks[idx + 1])
        m_new = jnp.maximum(m, jnp.max(s, axis=-1, keepdims=True))
        p = jnp.exp2(s - m_new).astype(BF16)
        acc = jnp.exp2(m - m_new) * acc + jnp.dot(p, c[1][c[2]:c[2] + c[3], :], preferred_element_type=F32)
        m = m_new
    o_ref[...] = (acc[:, :V_HEAD] / acc[:, V_HEAD:]).astype(o_ref.dtype)


def _attention(q, k, v1, batch, seq, nctx):
    tq = ATT_TQ
    nq = seq // tq
    ctx0 = batch * seq // nctx
    v1_w = v1.shape[-1]
    a_lat = pl.pallas_call(
        _attn_kernel,
        grid=(batch, MLA_HEADS, nq),
        in_specs=[
            pl.BlockSpec((None, tq, QK_DIM), lambda b, h, i: (h, b * nq + i, 0)),
            pl.BlockSpec((None, nctx, QK_DIM), lambda b, h, i: (h, ctx0 + b, 0)),
            pl.BlockSpec((None, nctx, v1_w), lambda b, h, i: (h, ctx0 + b, 0)),
            pl.BlockSpec((None, seq, QK_DIM), lambda b, h, i: (h, b, 0)),
            pl.BlockSpec((None, seq, v1_w), lambda b, h, i: (h, b, 0)),
        ],
        out_specs=pl.BlockSpec((tq, V_HEAD), lambda b, h, i: (b * nq + i, h)),
        out_shape=jax.ShapeDtypeStruct((batch * seq, MLA_W), BF16),
        compiler_params=_cparams("parallel", "parallel", "arbitrary"),
        name="attn_lat",
    )(q, k, v1, k, v1)
    a_ctx = pl.pallas_call(
        _attn_kernel,
        grid=(batch, MLA_HEADS),
        in_specs=[
            pl.BlockSpec((None, nctx, QK_DIM), lambda b, h: (h, ctx0 + b, 0)),
            pl.BlockSpec((None, nctx, QK_DIM), lambda b, h: (h, ctx0 + b, 0)),
            pl.BlockSpec((None, nctx, v1_w), lambda b, h: (h, ctx0 + b, 0)),
        ],
        out_specs=pl.BlockSpec((nctx, V_HEAD), lambda b, h: (b, h)),
        out_shape=jax.ShapeDtypeStruct((batch * nctx, MLA_W), BF16),
        compiler_params=_cparams("parallel", "parallel"),
        name="attn_ctx",
    )(q, k, v1)
    return a_lat, a_ctx


def _scan_blocks(batch, seq, nctx):
    ncb = nctx // TB
    nlb = seq // TB
    ctx0 = batch * nlb

    def fwd(b, j):
        return jnp.where(j < ncb, ctx0 + b * ncb + j, b * nlb + (j - ncb))

    def bwd(b, j):
        return jnp.where(j < ncb, ctx0 + b * ncb + (ncb - 1 - j), b * nlb + (nlb - 1 - (j - ncb)))

    return ncb + nlb, fwd, bwd


def _dot(a, b, dims=NN_DIMS):
    return lax.dot_general(a.astype(BF16), b.astype(BF16), dims, preferred_element_type=F32)


def _scan_units(nheads):
    return [(d, h) for d in range(2) for h in range(nheads)]


def _mlstm_kernel(xf_ref, xb_ref, ktf_ref, ktb_ref, gcf_ref, gcb_ref, grf_ref, grb_ref, hf_ref, hb_ref, c_ref, m_ref):
    @pl.when(pl.program_id(1) == 0)
    def _():
        c_ref[...] = jnp.zeros_like(c_ref)
        m_ref[...] = jnp.full(m_ref.shape, M_INIT, F32)

    nch = TB // CHUNK
    row = lax.broadcasted_iota(jnp.int32, (CHUNK, CHUNK), 0)
    col = lax.broadcasted_iota(jnp.int32, (CHUNK, CHUNK), 1)
    kscale = ML_DK ** -0.5
    qw = ML_HEADS * ML_DK
    refs = ((xf_ref, ktf_ref, gcf_ref, grf_ref, hf_ref), (xb_ref, ktb_ref, gcb_ref, grb_ref, hb_ref))
    units = _scan_units(ML_HEADS)

    ones = jnp.ones((CHUNK, LANES), BF16)
    sel_row = lax.broadcasted_iota(jnp.int32, (P_GATE, 2 * ML_HEADS * LANES), 0)
    sel_blk = lax.broadcasted_iota(jnp.int32, (P_GATE, 2 * ML_HEADS * LANES), 1) // LANES
    loc = {}
    for ci in range(nch):
        for d in range(2):
            x_ref, kt_ref, gc_ref, gr_ref, _ = refs[d]
            ch = ci if d == 0 else nch - 1 - ci
            rs = slice(ch * CHUNK, (ch + 1) * CHUNK)
            src_col = jnp.where(sel_blk < ML_HEADS, d * 8 + ML_HEADS + sel_blk, d * 8 + sel_blk - ML_HEADS)
            select = jnp.where(sel_row == src_col, 1.0, 0.0).astype(BF16)
            g = gc_ref[rs, :]
            g_hi = g.astype(BF16)
            g_lo = (g - g_hi.astype(F32)).astype(BF16)
            rep = (jnp.dot(g_hi, select, preferred_element_type=F32)
                   + jnp.dot(g_lo, select, preferred_element_type=F32))
            mask = (col <= row) if d == 0 else (col >= row)
            last = CHUNK - 1 if d == 0 else 0
            for h in range(ML_HEADS):
                b_rep = rep[:, h * LANES:(h + 1) * LANES]
                pm_rep = rep[:, (ML_HEADS + h) * LANES:(ML_HEADS + h + 1) * LANES]
                d_row = gr_ref[d * 8 + h:d * 8 + h + 1, rs]
                b_end = b_rep[last:last + 1, :]
                loc[ci, d, h] = dict(
                    rs=rs, b_rep=b_rep, b_end=b_end,
                    log_d=jnp.where(mask, b_rep[:, :CHUNK] + d_row, -jnp.inf),
                    log_w=b_end[:, :CHUNK] + d_row,
                    q=x_ref[rs, h * ML_DK:(h + 1) * ML_DK],
                    kt=kt_ref[h * ML_DK:(h + 1) * ML_DK, rs],
                    v1=jnp.concatenate([x_ref[rs, 2 * qw + h * ML_DV:2 * qw + (h + 1) * ML_DV], ones], axis=-1),
                    row_max=b_rep + pm_rep,
                    w_max=b_end + pm_rep[last:last + 1, :])
    for u in loc.values():
        u["qk"] = _dot(u["q"], u["kt"])

    state = {(d, h): (c_ref[d * ML_HEADS + h], m_ref[d * ML_HEADS + h]) for d, h in units}
    for ci in range(nch):
        cur = [(dh, loc[(ci,) + dh], state[dh]) for dh in units]
        for _, u, (cn, m) in cur:
            log_inter = u["b_rep"] + m
            m_t = jnp.maximum(log_inter, u["row_max"])
            u["w_inter"] = jnp.exp(log_inter - m_t)
            u["floor"] = jnp.exp(-m_t)
            u["s"] = u["qk"] * jnp.exp(u["log_d"] - m_t[:, :CHUNK]) * kscale
        for _, u, (cn, m) in cur:
            u["qc"] = _dot(u["q"], cn)
            u["sv"] = _dot(u["s"], u["v1"])
        for (d, h), u, (cn, m) in cur:
            num = u["w_inter"] * u["qc"][:, :ML_DV] + u["sv"][:, :ML_DV]
            den = u["w_inter"] * u["qc"][:, ML_DV:] + u["sv"][:, ML_DV:]
            refs[d][4][u["rs"], h * ML_DV:(h + 1) * ML_DV] = num / jnp.maximum(jnp.abs(den), u["floor"])
            m_new = jnp.maximum(u["b_end"] + m, u["w_max"])
            u["decay"] = jnp.exp(u["b_end"] + m - m_new)
            u["m_new"] = m_new
            u["kwt"] = u["kt"].astype(F32) * (jnp.exp(u["log_w"] - m_new[:, :CHUNK]) * kscale)
        for _, u, _st in cur:
            u["upd"] = _dot(u["kwt"], u["v1"])
        for dh, u, (cn, m) in cur:
            state[dh] = (jnp.concatenate([u["decay"], u["decay"]], axis=-1) * cn + u["upd"], u["m_new"])
    for d, h in units:
        st = d * ML_HEADS + h
        c_ref[st], m_ref[st] = state[(d, h)]


def _mlstm(p_ml, kt, gcol, grow, batch, seq, nctx):
    nt = p_ml.shape[0]
    nsteps, fwd, bwd = _scan_blocks(batch, seq, nctx)
    xw = 2 * ML_HEADS * ML_DK + ML_W
    nst = 2 * ML_HEADS
    return pl.pallas_call(
        _mlstm_kernel,
        grid=(batch, nsteps),
        in_specs=[
            pl.BlockSpec((TB, xw), lambda b, j: (fwd(b, j), 0)),
            pl.BlockSpec((TB, xw), lambda b, j: (bwd(b, j), 0)),
            pl.BlockSpec((ML_HEADS * ML_DK, TB), lambda b, j: (0, fwd(b, j))),
            pl.BlockSpec((ML_HEADS * ML_DK, TB), lambda b, j: (0, bwd(b, j))),
            pl.BlockSpec((TB, P_GATE), lambda b, j: (fwd(b, j), 0)),
            pl.BlockSpec((TB, P_GATE), lambda b, j: (bwd(b, j), 0)),
            pl.BlockSpec((P_GATE, TB), lambda b, j: (0, fwd(b, j))),
            pl.BlockSpec((P_GATE, TB), lambda b, j: (0, bwd(b, j))),
        ],
        out_specs=[
            pl.BlockSpec((TB, ML_W), lambda b, j: (fwd(b, j), 0)),
            pl.BlockSpec((TB, ML_W), lambda b, j: (bwd(b, j), 0)),
        ],
        out_shape=[jax.ShapeDtypeStruct((nt, ML_W), F32)] * 2,
        scratch_shapes=[
            pltpu.VMEM((nst, ML_DK, ML_DV + LANES), F32),
            pltpu.VMEM((nst, 1, LANES), F32),
        ],
        compiler_params=_cparams("parallel", "arbitrary"),
        name="mlstm",
    )(p_ml, p_ml, kt, kt, gcol, gcol, grow, grow)


INV_BLOCK = 16


def _unit_triangular_inverses(mats):
    n = mats[0].shape[0]
    row = lax.broadcasted_iota(jnp.int32, (n, n), 0)
    col = lax.broadcasted_iota(jnp.int32, (n, n), 1)
    eye = jnp.where(row == col, 1.0, 0.0)
    same_block = row // INV_BLOCK == col // INV_BLOCK
    ps = [jnp.where(same_block, a, 0.0) for a in mats]
    xs = [eye - p for p in ps]
    for _ in range((INV_BLOCK - 1).bit_length() - 1):
        ps = [_dot(p, p) for p in ps]
        xs = [x + xp for x, xp in zip(xs, [_dot(x, p) for x, p in zip(xs, ps)])]
    size = INV_BLOCK
    while size < n:
        sel = (row // (2 * size) == col // (2 * size)) & (row // size != col // size)
        ts = [_dot(x, jnp.where(sel, a, 0.0)) for x, a in zip(xs, mats)]
        xs = [x - tx for x, tx in zip(xs, [_dot(t, x) for t, x in zip(ts, xs)])]
        size *= 2
    return xs


def _gdn_kernel(xf_ref, xb_ref, gcf_ref, gcb_ref, grf_ref, grb_ref, of_ref, ob_ref, s_ref):
    @pl.when(pl.program_id(1) == 0)
    def _():
        s_ref[...] = jnp.zeros_like(s_ref)

    nch = TB // CHUNK
    row = lax.broadcasted_iota(jnp.int32, (CHUNK, CHUNK), 0)
    col = lax.broadcasted_iota(jnp.int32, (CHUNK, CHUNK), 1)
    nk = GD_HEADS * GD_DK
    refs = ((xf_ref, gcf_ref, grf_ref, of_ref), (xb_ref, gcb_ref, grb_ref, ob_ref))
    units = _scan_units(GD_HEADS)

    loc = {}
    for ci in range(nch):
        for d, h in units:
            x_ref, gc_ref, gr_ref, _ = refs[d]
            ch = ci if d == 0 else nch - 1 - ci
            rs = slice(ch * CHUNK, (ch + 1) * CHUNK)
            cb = 16 + d * 8 + h
            cg = 16 + d * 8 + 4 + h
            beta = jnp.broadcast_to(gc_ref[rs, cb:cb + 1], (CHUNK, LANES))
            g_col = jnp.broadcast_to(gc_ref[rs, cg:cg + 1], (CHUNK, LANES))
            g_row = gr_ref[cg:cg + 1, rs]
            g_end = g_col[CHUNK - 1:CHUNK, :] if d == 0 else g_col[0:1, :]
            incl = (col <= row) if d == 0 else (col >= row)
            q = x_ref[rs, h * GD_DK:(h + 1) * GD_DK]
            k = x_ref[rs, nk + h * GD_DK:nk + (h + 1) * GD_DK]
            v = x_ref[rs, 2 * nk + h * GD_DV:2 * nk + (h + 1) * GD_DV]
            kb = k * beta
            e_col = jnp.exp(g_col)
            loc[ci, d, h] = dict(
                rs=rs, k=k, kb=kb, q=q,
                strict=(col < row) if d == 0 else (col > row),
                decay=jnp.exp(jnp.where(incl, g_col[:, :CHUNK] - g_row, -jnp.inf)),
                rhs=jnp.concatenate([v * beta, kb * e_col], axis=-1),
                qg=q * e_col, kd=k * jnp.exp(g_end - g_col), e_end=jnp.exp(g_end))
    us = list(loc.values())
    for u in us:
        kk_qk = _dot(jnp.concatenate([u["kb"], u["q"]], axis=0), u["k"], NT_DIMS)
        u["kk"] = kk_qk[:CHUNK]
        u["attn"] = kk_qk[CHUNK:] * u["decay"]
    tinvs = _unit_triangular_inverses([jnp.where(u["strict"], u["kk"] * u["decay"], 0.0) for u in us])
    for u, tinv in zip(us, tinvs):
        u["uw"] = _dot(tinv, u["rhs"])

    state = {(d, h): s_ref[d * GD_HEADS + h] for d, h in units}
    for ci in range(nch):
        cur = [(dh, loc[(ci,) + dh]) for dh in units]
        for dh, u in cur:
            u["ws_qs"] = _dot(jnp.concatenate([u["uw"][:, GD_DV:], u["qg"]], axis=0), state[dh])
        for dh, u in cur:
            u["v_new"] = u["uw"][:, :GD_DV] - u["ws_qs"][:CHUNK]
            u["qs"] = u["ws_qs"][CHUNK:]
        for dh, u in cur:
            u["av"] = _dot(u["attn"], u["v_new"])
            u["upd"] = _dot(u["kd"], u["v_new"], TN_DIMS)
        for (d, h), u in cur:
            refs[d][3][u["rs"], h * GD_DV:(h + 1) * GD_DV] = u["qs"] + u["av"]
            state[(d, h)] = state[(d, h)] * u["e_end"] + u["upd"]
    for d, h in units:
        s_ref[d * GD_HEADS + h] = state[(d, h)]


def _gdn(qkv, gcol, grow, batch, seq, nctx):
    nt = qkv.shape[0]
    nsteps, fwd, bwd = _scan_blocks(batch, seq, nctx)
    cw = qkv.shape[1]
    return pl.pallas_call(
        _gdn_kernel,
        grid=(batch, nsteps),
        in_specs=[
            pl.BlockSpec((TB, cw), lambda b, j: (fwd(b, j), 0)),
            pl.BlockSpec((TB, cw), lambda b, j: (bwd(b, j), 0)),
            pl.BlockSpec((TB, P_GATE), lambda b, j: (fwd(b, j), 0)),
            pl.BlockSpec((TB, P_GATE), lambda b, j: (bwd(b, j), 0)),
            pl.BlockSpec((P_GATE, TB), lambda b, j: (0, fwd(b, j))),
            pl.BlockSpec((P_GATE, TB), lambda b, j: (0, bwd(b, j))),
        ],
        out_specs=[
            pl.BlockSpec((TB, GD_W), lambda b, j: (fwd(b, j), 0)),
            pl.BlockSpec((TB, GD_W), lambda b, j: (bwd(b, j), 0)),
        ],
        out_shape=[jax.ShapeDtypeStruct((nt, GD_W), F32)] * 2,
        scratch_shapes=[pltpu.VMEM((2 * GD_HEADS, GD_DK, GD_DV), F32)],
        compiler_params=_cparams("parallel", "arbitrary"),
        name="gdn",
    )(qkv, qkv, gcol, gcol, grow, grow)


def _head_rms(y, gain, nheads, width):
    parts = []
    for h in range(nheads):
        seg = y[:, h * width:(h + 1) * width]
        parts.append(_rms(seg))
    return jnp.concatenate(parts, axis=-1) * gain


def _merge_kernel(*refs, n_x, lat_tiles):
    x_refs = refs[:n_x]
    (mod_ref, al_ref, ac_ref, hf_ref, hb_ref, o_ref, gf_ref, gb_ref, z_ref,
     an_ref, mn_ref, gn_ref, w_ref, out_ref) = refs[n_x:]
    a = _token_tile((al_ref, ac_ref), lat_tiles).astype(F32)
    ya = _head_rms(a, an_ref[...], MLA_HEADS, V_HEAD)
    ym = _head_rms(hf_ref[...] + hb_ref[...], mn_ref[...], ML_HEADS, ML_DV) * _sigmoid(o_ref[...].astype(F32))
    z = z_ref[...].astype(F32)
    yg = _head_rms(gf_ref[...] + gb_ref[...], gn_ref[...], GD_HEADS, GD_DV) * (z * _sigmoid(z))
    y = jnp.dot(ya.astype(BF16), w_ref[0:MLA_W, :], preferred_element_type=F32)
    y = y + jnp.dot(ym.astype(BF16), w_ref[MLA_W:MLA_W + ML_W, :], preferred_element_type=F32)
    y = y + jnp.dot(yg.astype(BF16), w_ref[MLA_W + ML_W:, :], preferred_element_type=F32)
    out_ref[...] = _token_tile(x_refs, lat_tiles) + mod_ref[2:3, :] * y


def _merge(l, xs, mod, a_lat, a_ctx, hf, hb, p_ml, gf, gb, gd_z, mla_norm, ml_norm, gd_norm, w_out, seq, batch, nrows):
    d = xs[0].shape[1]
    lat_tiles = a_lat.shape[0] // TM
    assert a_ctx.shape[0] == TM
    row = functools.partial(_mod_row, seq=seq, batch=batch)
    tile = lambda w, cb=0: pl.BlockSpec((TM, w), lambda i: (i, cb))
    vec = lambda w: pl.BlockSpec((None, 1, w), lambda i: (l, 0, 0))
    return pl.pallas_call(
        functools.partial(_merge_kernel, n_x=len(xs), lat_tiles=lat_tiles),
        grid=(nrows // TM,),
        in_specs=_token_specs(xs, d) + [
            pl.BlockSpec((None, None, 6, d), lambda i: (l, row(i), 0, 0)),
            *_token_specs((a_lat, a_ctx), MLA_W),
            tile(ML_W), tile(ML_W), tile(ML_W, 2),
            tile(GD_W), tile(GD_W), tile(GD_W),
            vec(MLA_W), vec(ML_W), vec(GD_W),
            pl.BlockSpec((None, MIX_W, d), lambda i: (l, 0, 0)),
        ],
        out_specs=tile(d),
        out_shape=jax.ShapeDtypeStruct((nrows, d), F32),
        compiler_params=_cparams("parallel"),
        name="merge",
    )(*xs, mod, a_lat, a_ctx, hf, hb, p_ml, gf, gb, gd_z, mla_norm[:, None, :], ml_norm[:, None, :],
      jnp.tile(gd_norm, (1, GD_HEADS))[:, None, :], w_out)


def _mlp_kernel(x_ref, mod_ref, nw_ref, w1_ref, w2_ref, *rest):
    out_ref, h_ref, acc_ref = rest[-3:]
    j = pl.program_id(1)

    @pl.when(j == 0)
    def _():
        h_ref[...] = _norm_mod(x_ref[...], nw_ref[...], mod_ref[3:4, :], mod_ref[4:5, :]).astype(BF16)
        acc_ref[...] = jnp.zeros_like(acc_ref)

    a = jnp.maximum(jnp.dot(h_ref[...], w1_ref[...], preferred_element_type=F32), 0.0)
    acc_ref[...] += jnp.dot((a * a).astype(BF16), w2_ref[...], preferred_element_type=F32)

    @pl.when(j == pl.num_programs(1) - 1)
    def _():
        y = x_ref[...] + mod_ref[5:6, :] * acc_ref[...]
        out_ref[...] = _rms(y) * rest[0][...] if len(rest) == 4 else y


def _mlp(l, xg, mod, norm_w, w1, w2, seq, batch, nrows, final_w=None):
    d = xg.shape[1]
    f = w1.shape[-1]
    tf = min(MLP_TF, f)
    row = functools.partial(_mod_row, seq=seq, batch=batch)
    extra = [] if final_w is None else [final_w.reshape(1, d)]
    return pl.pallas_call(
        _mlp_kernel,
        grid=(nrows // TM, f // tf),
        in_specs=[
            pl.BlockSpec((TM, d), lambda i, j: (i, 0)),
            pl.BlockSpec((None, None, 6, d), lambda i, j: (l, row(i), 0, 0)),
            pl.BlockSpec((None, 1, d), lambda i, j: (l, 0, 0)),
            pl.BlockSpec((None, d, tf), lambda i, j: (l, 0, j)),
            pl.BlockSpec((None, tf, d), lambda i, j: (l, j, 0)),
        ] + [pl.BlockSpec((1, d), lambda i, j: (0, 0)) for _ in extra],
        out_specs=pl.BlockSpec((TM, d), lambda i, j: (i, 0)),
        out_shape=jax.ShapeDtypeStruct((nrows, d), F32),
        scratch_shapes=[pltpu.VMEM((TM, d), BF16), pltpu.VMEM((TM, d), F32)],
        compiler_params=_cparams("parallel", "arbitrary"),
        name="mlp",
    )(xg, mod, norm_w[:, None, :], w1, w2, *extra)


def _regroup_w_in_kernel(wt_ref, o_ref):
    o_ml = Q_LORA + KV_LORA + QK_ROPE
    o_mlg = o_ml + P_ML
    o_gd = o_mlg + 4 * ML_HEADS
    o_gdg = o_gd + P_GD
    tr = wt_ref.shape[1]
    eye = jnp.where(lax.broadcasted_iota(jnp.int32, (tr, tr), 0) == lax.broadcasted_iota(jnp.int32, (tr, tr), 1),
                    1.0, 0.0).astype(BF16)
    zeros = lambda n: jnp.zeros((tr, n), BF16)
    piece = lambda a, b: lax.dot_general(eye, wt_ref[a:b, :].astype(BF16), NT_DIMS,
                                         preferred_element_type=F32).astype(BF16)
    o_ref[...] = jnp.concatenate([
        piece(0, o_ml), zeros(P_MLA - o_ml),
        piece(o_ml, o_mlg),
        piece(o_gd, o_gdg),
        piece(o_mlg, o_gd), piece(o_gdg, o_gdg + 4 * GD_HEADS), zeros(P_GATE - 4 * ML_HEADS - 4 * GD_HEADS),
    ], axis=-1)


def _regroup_w_in(w_in):
    nl, d, n_in = w_in.shape
    tr = min(512, d)
    return pl.pallas_call(
        _regroup_w_in_kernel,
        grid=(nl, d // tr),
        in_specs=[pl.BlockSpec((None, n_in, tr), lambda l, i: (l, 0, i))],
        out_specs=pl.BlockSpec((None, tr, P_ALL), lambda l, i: (l, i, 0)),
        out_shape=jax.ShapeDtypeStruct((nl, d, P_ALL), BF16),
        compiler_params=_cparams("parallel", "parallel"),
        name="regroup_w_in",
    )(jnp.swapaxes(w_in, 1, 2))


def _regroup_heads(w, first):
    nl, kdim, _ = w.shape
    w4 = w.reshape(nl, kdim, MLA_HEADS, -1)
    return jnp.concatenate([w4[..., :first].reshape(nl, kdim, -1), w4[..., first:].reshape(nl, kdim, -1)],
                           axis=-1).astype(BF16)


def _rope_tables(seq):
    half = QK_ROPE // 2
    t = jnp.arange(seq, dtype=jnp.int32)
    inv = ROPE_BASE ** (-jnp.arange(0, half, 2, dtype=F32) / half)
    ang_r = (t // GRID_W).astype(F32)[:, None] * inv
    ang_c = (t % GRID_W).astype(F32)[:, None] * inv
    cos = jnp.concatenate([jnp.cos(ang_r)] * 2 + [jnp.cos(ang_c)] * 2, axis=-1)
    sin = jnp.concatenate([-jnp.sin(ang_r), jnp.sin(ang_r), -jnp.sin(ang_c), jnp.sin(ang_c)], axis=-1)
    cos = jnp.concatenate([cos, jnp.ones((TM, QK_ROPE), F32)], axis=0)
    sin = jnp.concatenate([sin, jnp.zeros((TM, QK_ROPE), F32)], axis=0)
    return jnp.tile(cos, (1, MLA_HEADS)), jnp.tile(sin, (1, MLA_HEADS))


def _gate_params(ml_gate_bias, gd_a_log, gd_dt_bias):
    nl = ml_gate_bias.shape[0]
    zh = jnp.zeros((nl, GD_HEADS), F32)
    bias = jnp.concatenate([ml_gate_bias, zh, gd_dt_bias[:, 0], zh, gd_dt_bias[:, 1]], axis=-1)
    alog = jnp.concatenate([jnp.zeros((nl, 4 * ML_HEADS), F32), zh, gd_a_log[:, 0], zh, gd_a_log[:, 1]], axis=-1)
    par = jnp.stack([bias, alog], axis=1)
    return jnp.pad(par, ((0, 0), (0, 6), (0, P_GATE - par.shape[-1])))


def kernel(x, c, ctx, c_ctx, w_ada, b_ada, norm1, norm2, w_in, mla_q_norm, mla_w_uq, mla_kv_norm, mla_w_ukv, mla_out_norm, ml_gate_bias, ml_out_norm, gd_conv, gd_a_log, gd_dt_bias, gd_out_norm, w_out, w_mlp1, w_mlp2, final_norm):
    batch, seq, d = x.shape
    nctx = ctx.shape[1]
    depth = w_ada.shape[0]
    n_lat = batch * seq
    n_all = n_lat + batch * nctx
    assert seq % TM == 0 and (batch * nctx) % TM == 0 and nctx % TB == 0 and TM % nctx == 0

    xs = (x.reshape(n_lat, d), ctx.reshape(batch * nctx, d))
    mod_all = _ada(jnp.concatenate([c, c_ctx[None]], axis=0), w_ada, b_ada).reshape(depth, batch + 1, 6, d)
    w_in_g = _regroup_w_in(w_in)
    wuq_g = _regroup_heads(mla_w_uq, QK_NOPE)
    wukv_g = _regroup_heads(mla_w_ukv, QK_NOPE)
    w_out_b = w_out.astype(BF16)
    w1_b = w_mlp1.astype(BF16)
    w2_b = w_mlp2.astype(BF16)
    cos_t, sin_t = _rope_tables(seq)
    gate_par = _gate_params(ml_gate_bias, gd_a_log, gd_dt_bias)

    for l in range(depth):
        last = l == depth - 1
        p_mla, p_ml, ml_kt, gd_qkv, gd_z, gcol, grow = _inproj(l, xs, mod_all, norm1, w_in_g, gate_par, gd_conv,
                                                               seq, batch, nctx)
        q, k, v1 = _mla_qkv_call(l, p_mla, mla_q_norm, mla_kv_norm, wuq_g, wukv_g, cos_t, sin_t, seq, n_lat)
        a_lat, a_ctx = _attention(q, k, v1, batch, seq, nctx)
        hf, hb = _mlstm(p_ml, ml_kt, gcol, grow, batch, seq, nctx)
        gf, gb = _gdn(gd_qkv, gcol, grow, batch, seq, nctx)
        nrows = n_lat if last else n_all
        xg = _merge(l, xs, mod_all, a_lat, a_ctx, hf, hb, p_ml, gf, gb, gd_z, mla_out_norm, ml_out_norm,
                    gd_out_norm, w_out_b, seq, batch, nrows)
        xs = (_mlp(l, xg, mod_all, norm2, w1_b, w2_b, seq, batch, nrows, final_norm if last else None),)
    return xs[0].reshape(batch, seq, d)
```

```python
import functools

import jax
import jax.numpy as jnp
from jax import lax
from jax.experimental import pallas as pl
from jax.experimental.pallas import tpu as pltpu

F32 = jnp.float32
BF16 = jnp.bfloat16

EPS = 1e-6
LOG2_E = 1.4426950408889634
M_INIT = -1e30
GRID_W = 64
ROPE_BASE = 10000.0

MLA_HEADS = 8
Q_LORA = 512
KV_LORA = 256
QK_NOPE = 128
QK_ROPE = 64
V_HEAD = 128
QK_DIM = QK_NOPE + QK_ROPE
ML_HEADS = 4
ML_DK = 64
ML_DV = 128
GD_HEADS = 4
GD_DK = 128
GD_DV = 128
CHUNK = 64
CONV_W = 5

MLA_W = MLA_HEADS * V_HEAD
ML_W = ML_HEADS * ML_DV
GD_W = GD_HEADS * GD_DV
MIX_W = MLA_W + ML_W + GD_W

P_MLA = 896
P_ML = 1536
P_GD = 2048
P_GATE = 128
P_ALL = P_MLA + P_ML + P_GD + P_GATE

TM = 512
MLP_TF = 2048
ADA_COLS = 1024
TB = 256
HALO = 16
GATE_ROWS = 256
LANES = 128

VMEM_LIMIT_V7X = 58 * 1024 * 1024

NT_DIMS = (((1,), (1,)), ((), ()))
TN_DIMS = (((0,), (0,)), ((), ()))
NN_DIMS = (((1,), (0,)), ((), ()))


def _cparams(*sem):
    return pltpu.CompilerParams(dimension_semantics=sem, vmem_limit_bytes=VMEM_LIMIT_V7X)


def _sigmoid(x):
    return 1.0 / (1.0 + jnp.exp(-x))


def _rms(x):
    return x * lax.rsqrt(jnp.mean(x * x, axis=-1, keepdims=True) + EPS)


def _norm_mod(x, norm_w, shift, scale):
    return _rms(x) * norm_w * (1.0 + scale) + shift


def _mod_row(i, seq, batch):
    return jnp.minimum((i * TM) // seq, batch)


def _ada_kernel(cb_ref, w_ref, b_ref, o_ref, s_ref):
    @pl.when((pl.program_id(0) == 0) & (pl.program_id(1) == 0))
    def _():
        cb = cb_ref[...]
        s_ref[...] = cb * _sigmoid(cb)

    rows, d = cb_ref.shape[0], cb_ref.shape[1]
    tn = w_ref.shape[-1]
    tc = next(t for t in (ADA_COLS, ADA_COLS // 2, LANES) if tn % t == 0)
    sub = 8
    for c0 in range(0, tn, tc):
        def body(kb, accs):
            k0 = pl.multiple_of(kb * sub, sub)
            w = w_ref[pl.ds(k0, sub), c0:c0 + tc]
            return tuple(acc + w * jnp.tile(s_ref[r, pl.ds(k0, sub), :], (1, tc // LANES))
                         for r, acc in enumerate(accs))

        accs = lax.fori_loop(0, d // sub, body, tuple(jnp.zeros((sub, tc), F32) for _ in range(rows)), unroll=4)
        for r in range(rows):
            o_ref[r:r + 1, c0:c0 + tc] = jnp.sum(accs[r], axis=0, keepdims=True) + b_ref[:, c0:c0 + tc]


def _ada(cvec, w_ada, b_ada):
    nl, d, n6 = w_ada.shape
    r = cvec.shape[0]
    tn = next(t for t in (2048, 1536, 1024, 512, LANES) if n6 % t == 0)
    cb = jnp.broadcast_to(cvec[:, :, None], (r, d, LANES))
    return pl.pallas_call(
        _ada_kernel,
        grid=(nl, n6 // tn),
        in_specs=[
            pl.BlockSpec((r, d, LANES), lambda l, j: (0, 0, 0)),
            pl.BlockSpec((None, d, tn), lambda l, j: (l, 0, j)),
            pl.BlockSpec((None, 1, tn), lambda l, j: (l, 0, j)),
        ],
        out_specs=pl.BlockSpec((None, r, tn), lambda l, j: (l, 0, j)),
        out_shape=jax.ShapeDtypeStruct((nl, r, n6), F32),
        scratch_shapes=[pltpu.VMEM((r, d, LANES), F32)],
        compiler_params=_cparams("arbitrary", "arbitrary"),
        name="ada",
    )(cb, w_ada, b_ada.reshape(nl, 1, n6))


def _token_specs(xs, width):
    if len(xs) == 1:
        return [pl.BlockSpec((TM, width), lambda i: (i, 0))]
    nl = xs[0].shape[0] // TM
    return [pl.BlockSpec((TM, width), lambda i: (jnp.minimum(i, nl - 1), 0)),
            pl.BlockSpec((TM, width), lambda i: (jnp.maximum(i - nl, 0), 0))]


def _token_tile(x_refs, lat_tiles):
    if len(x_refs) == 1:
        return x_refs[0][...]
    return jnp.where(pl.program_id(0) < lat_tiles, x_refs[0][...], x_refs[1][...])


def _gate_columns(g, bias, a_log):
    tb = g.shape[0]
    z = g + bias
    neg_a = -jnp.exp(a_log)
    lane = lax.broadcasted_iota(jnp.int32, z.shape, 1)
    kind = (lane // 4) % 4
    is_ml = lane < 16
    is_cum = (kind % 2 == 1) & (lane < 32)
    is_bwd = kind >= 2
    soft = jnp.log(1.0 + jnp.exp(-jnp.abs(z)))
    log_sig = jnp.minimum(z, 0.0) - soft
    softplus = jnp.maximum(z, 0.0) + soft
    pre = jnp.where(is_ml, jnp.where(is_cum, log_sig, z), jnp.where(is_cum, neg_a * softplus, _sigmoid(z)))
    r = lax.broadcasted_iota(jnp.int32, (tb, tb), 0)
    c = lax.broadcasted_iota(jnp.int32, (tb, tb), 1)
    same = (r // CHUNK) == (c // CHUNK)
    lower = jnp.where(same & (c <= r), 1.0, 0.0).astype(BF16)
    upper = jnp.where(same & (c >= r), 1.0, 0.0).astype(BF16)
    hi = pre.astype(BF16)
    rest = pre - hi.astype(F32)
    mid = rest.astype(BF16)
    parts = (hi, mid, (rest - mid.astype(F32)).astype(BF16))
    cum_f = sum(jnp.dot(lower, p, preferred_element_type=F32) for p in parts)
    cum_b = sum(jnp.dot(upper, p, preferred_element_type=F32) for p in parts)
    out = jnp.where(is_cum, jnp.where(is_bwd, cum_b, cum_f), pre)
    is_mli = is_ml & jnp.logical_not(is_cum)
    d = out - pltpu.roll(out, P_GATE - ML_HEADS, 1)
    t = lax.broadcasted_iota(jnp.int32, z.shape, 0) % CHUNK
    run = d
    shift = 1
    while shift < CHUNK:
        prev_f = jnp.where(t >= shift, pltpu.roll(run, shift, 0), -jnp.inf)
        prev_b = jnp.where(t < CHUNK - shift, pltpu.roll(run, tb - shift, 0), -jnp.inf)
        run = jnp.maximum(run, jnp.where(is_bwd, prev_b, prev_f))
        shift *= 2
    return jnp.where(is_mli, run, out), jnp.where(is_mli, d, out).T


def _short_conv_heads(buf_ref, cw_ref, tap_ok):
    half = CONV_W // 2
    y = None
    for j in range(CONV_W):
        term = buf_ref[HALO - half + j:HALO - half + j + TM, :] * cw_ref[j:j + 1, :]
        if tap_ok[j] is not None:
            term = term * tap_ok[j]
        y = term if y is None else y + term
    y = y * _sigmoid(y)
    nk = GD_HEADS * GD_DK
    parts = []
    for h in range(2 * GD_HEADS):
        seg = y[:, h * GD_DK:(h + 1) * GD_DK]
        seg = seg * lax.rsqrt(jnp.sum(seg * seg, axis=-1, keepdims=True) + EPS)
        parts.append(seg * (GD_DK ** -0.5) if h < GD_HEADS else seg)
    return jnp.concatenate(parts + [y[:, 2 * nk:]], axis=-1)


def _inproj_kernel(*refs, n_x, lat_tiles, seq_tiles, nctx):
    x_refs = refs[:n_x]
    (xp_ref, xn_ref, mod_ref, nw_ref, w_ref, gpar_ref, cw_ref,
     omla_ref, oml_ref, kt_ref, gdq_ref, z_ref, gcol_ref, grow_ref, buf_ref) = refs[n_x:]
    i = pl.program_id(0)
    in_lat = i < lat_tiles
    first = jnp.logical_or(jnp.logical_not(in_lat), i % seq_tiles == 0)
    last = jnp.logical_or(jnp.logical_not(in_lat), i % seq_tiles == seq_tiles - 1)

    x_ext = jnp.concatenate([xp_ref[...], _token_tile(x_refs, lat_tiles), xn_ref[...]], axis=0)
    h_ext = _norm_mod(x_ext, nw_ref[...], mod_ref[0:1, :], mod_ref[1:2, :]).astype(BF16)
    h = h_ext[HALO:HALO + TM]
    proj = lambda lhs, c0, n: jnp.dot(lhs, w_ref[:, c0:c0 + n], preferred_element_type=F32)

    c_gd = P_MLA + P_ML
    qkv_w = 2 * GD_HEADS * GD_DK + GD_W
    p_qkv = proj(h_ext, c_gd, qkv_w)
    buf_ref[0:HALO, :] = jnp.where(first, 0.0, p_qkv[0:HALO])
    buf_ref[HALO:HALO + TM, :] = p_qkv[HALO:HALO + TM]
    buf_ref[HALO + TM:, :] = jnp.where(last, 0.0, p_qkv[HALO + TM:])
    gates = proj(h, c_gd + P_GD, P_GATE)
    omla_ref[...] = proj(h, 0, P_MLA).astype(omla_ref.dtype)
    p_ml = proj(h, P_MLA, P_ML)
    oml_ref[...] = p_ml.astype(oml_ref.dtype)
    kw = ML_HEADS * ML_DK
    kt_ref[...] = p_ml[:, kw:2 * kw].T.astype(kt_ref.dtype)
    z_ref[...] = proj(h, c_gd + qkv_w, GD_W).astype(z_ref.dtype)

    pos = lax.broadcasted_iota(jnp.int32, (TM, LANES), 0) % nctx
    reps = qkv_w // LANES
    tap_ok = []
    for j in range(CONV_W):
        o = j - CONV_W // 2
        crosses = (pos + o < 0) | (pos + o >= nctx)
        keep = jnp.where(jnp.logical_or(in_lat, jnp.logical_not(crosses)), 1.0, 0.0)
        tap_ok.append(None if o == 0 or TM <= nctx else jnp.tile(keep, (1, reps)))
    gdq_ref[...] = _short_conv_heads(buf_ref, cw_ref, tap_ok).astype(gdq_ref.dtype)
    for r0 in range(0, TM, GATE_ROWS):
        col, row_t = _gate_columns(gates[r0:r0 + GATE_ROWS], gpar_ref[0:1, :], gpar_ref[1:2, :])
        gcol_ref[r0:r0 + GATE_ROWS, :] = col
        grow_ref[:, r0:r0 + GATE_ROWS] = row_t


def _inproj(l, xs, mod, norm_w, w, gate_par, conv_w, seq, batch, nctx):
    nt = sum(x.shape[0] for x in xs)
    d = xs[0].shape[1]
    row = functools.partial(_mod_row, seq=seq, batch=batch)
    lat_tiles = batch * seq // TM
    hpt = TM // HALO
    n_halo = xs[0].shape[0] // HALO
    qkv_w = 2 * GD_HEADS * GD_DK + GD_W
    kw = ML_HEADS * ML_DK
    tile = lambda n: pl.BlockSpec((TM, n), lambda i: (i, 0))
    outs = [
        (tile(P_MLA), (nt, P_MLA), BF16),
        (tile(P_ML), (nt, P_ML), BF16),
        (pl.BlockSpec((kw, TM), lambda i: (0, i)), (kw, nt), BF16),
        (tile(qkv_w), (nt, qkv_w), BF16),
        (tile(GD_W), (nt, GD_W), BF16),
        (tile(P_GATE), (nt, P_GATE), F32),
        (pl.BlockSpec((P_GATE, TM), lambda i: (0, i)), (P_GATE, nt), F32),
    ]
    kern = functools.partial(_inproj_kernel, n_x=len(xs), lat_tiles=lat_tiles, seq_tiles=seq // TM, nctx=nctx)
    return pl.pallas_call(
        kern,
        grid=(nt // TM,),
        in_specs=_token_specs(xs, d) + [
            pl.BlockSpec((HALO, d), lambda i: (jnp.clip(i * hpt - 1, 0, n_halo - 1), 0)),
            pl.BlockSpec((HALO, d), lambda i: (jnp.clip((i + 1) * hpt, 0, n_halo - 1), 0)),
            pl.BlockSpec((None, None, 6, d), lambda i: (l, row(i), 0, 0)),
            pl.BlockSpec((None, 1, d), lambda i: (l, 0, 0)),
            pl.BlockSpec((None, d, P_ALL), lambda i: (l, 0, 0), pipeline_mode=pl.Buffered(1)),
            pl.BlockSpec((None, 8, P_GATE), lambda i: (l, 0, 0)),
            pl.BlockSpec((None, CONV_W, qkv_w), lambda i: (l, 0, 0)),
        ],
        out_specs=[o[0] for o in outs],
        out_shape=[jax.ShapeDtypeStruct(o[1], o[2]) for o in outs],
        scratch_shapes=[pltpu.VMEM((TM + 2 * HALO, qkv_w), F32)],
        compiler_params=_cparams("parallel"),
        name="inproj",
    )(*xs, xs[0], xs[0], mod, norm_w[:, None, :], w, gate_par, conv_w)


def _swap16(x):
    n = x.shape[-1]
    lane = lax.broadcasted_iota(jnp.int32, x.shape, 1)
    return jnp.where(lane % 32 < 16, pltpu.roll(x, n - 16, 1), pltpu.roll(x, 16, 1))


def _mla_qkv_kernel(p_ref, qn_ref, kvn_ref, wuq_ref, wukv_ref, cos_ref, sin_ref, q_ref, k_ref, v_ref):
    p = p_ref[...].astype(F32)
    cq = p[:, :Q_LORA]
    ckv = p[:, Q_LORA:Q_LORA + KV_LORA]
    kpe = p[:, Q_LORA + KV_LORA:]
    cos = cos_ref[...]
    sin = sin_ref[...]
    qn = (_rms(cq) * qn_ref[...]).astype(BF16)
    q = jnp.dot(qn, wuq_ref[...], preferred_element_type=F32) * (QK_DIM ** -0.5 * LOG2_E)
    nope_w = MLA_HEADS * QK_NOPE
    qpe = q[:, nope_w:]
    qpe = qpe * cos + _swap16(qpe) * sin
    kvn = (_rms(ckv) * kvn_ref[...]).astype(BF16)
    kv = jnp.dot(kvn, wukv_ref[...], preferred_element_type=F32)
    kpe = kpe * cos[:, :LANES] + _swap16(kpe) * sin[:, :LANES]
    kpe = kpe[:, :QK_ROPE]
    ones = jnp.ones((q.shape[0], LANES), BF16)
    for h in range(MLA_HEADS):
        qh = jnp.concatenate([q[:, h * QK_NOPE:(h + 1) * QK_NOPE], qpe[:, h * QK_ROPE:(h + 1) * QK_ROPE]], axis=-1)
        kh = jnp.concatenate([kv[:, h * QK_NOPE:(h + 1) * QK_NOPE], kpe], axis=-1)
        q_ref[h] = qh.astype(BF16)
        k_ref[h] = kh.astype(BF16)
        v_ref[h, :, :V_HEAD] = kv[:, nope_w + h * V_HEAD:nope_w + (h + 1) * V_HEAD].astype(BF16)
        v_ref[h, :, V_HEAD:] = ones


def _mla_qkv_call(l, p_mla, q_norm, kv_norm, wuq, wukv, cos_t, sin_t, seq, n_lat_rows):
    nt = p_mla.shape[0]
    lat_tiles = seq // TM
    nl_tiles = n_lat_rows // TM
    pe_w = MLA_HEADS * QK_ROPE
    v1_w = V_HEAD + LANES

    def rope_block(i):
        return (jnp.where(i < nl_tiles, i % lat_tiles, lat_tiles), 0)

    return pl.pallas_call(
        _mla_qkv_kernel,
        grid=(nt // TM,),
        in_specs=[
            pl.BlockSpec((TM, P_MLA), lambda i: (i, 0)),
            pl.BlockSpec((None, 1, Q_LORA), lambda i: (l, 0, 0)),
            pl.BlockSpec((None, 1, KV_LORA), lambda i: (l, 0, 0)),
            pl.BlockSpec((None,) + wuq.shape[1:], lambda i: (l, 0, 0)),
            pl.BlockSpec((None,) + wukv.shape[1:], lambda i: (l, 0, 0)),
            pl.BlockSpec((TM, pe_w), rope_block),
            pl.BlockSpec((TM, pe_w), rope_block),
        ],
        out_specs=[
            pl.BlockSpec((MLA_HEADS, TM, QK_DIM), lambda i: (0, i, 0)),
            pl.BlockSpec((MLA_HEADS, TM, QK_DIM), lambda i: (0, i, 0)),
            pl.BlockSpec((MLA_HEADS, TM, v1_w), lambda i: (0, i, 0)),
        ],
        out_shape=[
            jax.ShapeDtypeStruct((MLA_HEADS, nt, QK_DIM), BF16),
            jax.ShapeDtypeStruct((MLA_HEADS, nt, QK_DIM), BF16),
            jax.ShapeDtypeStruct((MLA_HEADS, nt, v1_w), BF16),
        ],
        compiler_params=_cparams("parallel"),
        name="mla_qkv",
    )(p_mla, q_norm[:, None, :], kv_norm[:, None, :], wuq, wukv, cos_t, sin_t)


KV_CHUNK = 512
ATT_TQ = 1024


def _attn_kernel(q_ref, *refs):
    o_ref = refs[-1]
    q = q_ref[...]
    chunks = []
    for k_ref, v_ref in zip(refs[0:-1:2], refs[1:-1:2]):
        n = k_ref.shape[0]
        step = min(KV_CHUNK, n)
        chunks += [(k_ref, v_ref, c0, step) for c0 in range(0, n, step)]
    scores = lambda c: lax.dot_general(q, c[0][c[2]:c[2] + c[3], :], NT_DIMS, preferred_element_type=F32)
    m = jnp.full((q.shape[0], 1), -jnp.inf, F32)
    acc = jnp.zeros((q.shape[0], V_HEAD + LANES), F32)
    s_next = scores(chunks[0])
    for idx, c in enumerate(chunks):
        s = s_next
        if idx + 1 < len(chunks):
            s_next = scores(chunks[idx + 1])
        m_new = jnp.maximum(m, jnp.max(s, axis=-1, keepdims=True))
        p = jnp.exp2(s - m_new).astype(BF16)
        acc = jnp.exp2(m - m_new) * acc + jnp.dot(p, c[1][c[2]:c[2] + c[3], :], preferred_element_type=F32)
        m = m_new
    o_ref[...] = (acc[:, :V_HEAD] / acc[:, V_HEAD:]).astype(o_ref.dtype)


def _attention(q, k, v1, batch, seq, nctx):
    tq = ATT_TQ
    nq = seq // tq
    ctx0 = batch * seq // nctx
    v1_w = v1.shape[-1]
    a_lat = pl.pallas_call(
        _attn_kernel,
        grid=(batch, MLA_HEADS, nq),
        in_specs=[
            pl.BlockSpec((None, tq, QK_DIM), lambda b, h, i: (h, b * nq + i, 0)),
            pl.BlockSpec((None, nctx, QK_DIM), lambda b, h, i: (h, ctx0 + b, 0)),
            pl.BlockSpec((None, nctx, v1_w), lambda b, h, i: (h, ctx0 + b, 0)),
            pl.BlockSpec((None, seq, QK_DIM), lambda b, h, i: (h, b, 0)),
            pl.BlockSpec((None, seq, v1_w), lambda b, h, i: (h, b, 0)),
        ],
        out_specs=pl.BlockSpec((tq, V_HEAD), lambda b, h, i: (b * nq + i, h)),
        out_shape=jax.ShapeDtypeStruct((batch * seq, MLA_W), BF16),
        compiler_params=_cparams("parallel", "parallel", "arbitrary"),
        name="attn_lat",
    )(q, k, v1, k, v1)
    a_ctx = pl.pallas_call(
        _attn_kernel,
        grid=(batch, MLA_HEADS),
        in_specs=[
            pl.BlockSpec((None, nctx, QK_DIM), lambda b, h: (h, ctx0 + b, 0)),
            pl.BlockSpec((None, nctx, QK_DIM), lambda b, h: (h, ctx0 + b, 0)),
            pl.BlockSpec((None, nctx, v1_w), lambda b, h: (h, ctx0 + b, 0)),
        ],
        out_specs=pl.BlockSpec((nctx, V_HEAD), lambda b, h: (b, h)),
        out_shape=jax.ShapeDtypeStruct((batch * nctx, MLA_W), BF16),
        compiler_params=_cparams("parallel", "parallel"),
        name="attn_ctx",
    )(q, k, v1)
    return a_lat, a_ctx


def _scan_blocks(batch, seq, nctx):
    ncb = nctx // TB
    nlb = seq // TB
    ctx0 = batch * nlb

    def fwd(b, j):
        return jnp.where(j < ncb, ctx0 + b * ncb + j, b * nlb + (j - ncb))

    def bwd(b, j):
        return jnp.where(j < ncb, ctx0 + b * ncb + (ncb - 1 - j), b * nlb + (nlb - 1 - (j - ncb)))

    return ncb + nlb, fwd, bwd


def _dot(a, b, dims=NN_DIMS):
    return lax.dot_general(a.astype(BF16), b.astype(BF16), dims, preferred_element_type=F32)


def _scan_units(nheads):
    return [(d, h) for d in range(2) for h in range(nheads)]


def _mlstm_kernel(xf_ref, xb_ref, ktf_ref, ktb_ref, gcf_ref, gcb_ref, grf_ref, grb_ref, hf_ref, hb_ref, c_ref, m_ref):
    @pl.when(pl.program_id(1) == 0)
    def _():
        c_ref[...] = jnp.zeros_like(c_ref)
        m_ref[...] = jnp.full(m_ref.shape, M_INIT, F32)

    nch = TB // CHUNK
    row = lax.broadcasted_iota(jnp.int32, (CHUNK, CHUNK), 0)
    col = lax.broadcasted_iota(jnp.int32, (CHUNK, CHUNK), 1)
    kscale = ML_DK ** -0.5
    qw = ML_HEADS * ML_DK
    refs = ((xf_ref, ktf_ref, gcf_ref, grf_ref, hf_ref), (xb_ref, ktb_ref, gcb_ref, grb_ref, hb_ref))
    units = _scan_units(ML_HEADS)

    ones = jnp.ones((CHUNK, LANES), BF16)
    sel_row = lax.broadcasted_iota(jnp.int32, (P_GATE, 2 * ML_HEADS * LANES), 0)
    sel_blk = lax.broadcasted_iota(jnp.int32, (P_GATE, 2 * ML_HEADS * LANES), 1) // LANES
    loc = {}
    for ci in range(nch):
        for d in range(2):
            x_ref, kt_ref, gc_ref, gr_ref, _ = refs[d]
            ch = ci if d == 0 else nch - 1 - ci
            rs = slice(ch * CHUNK, (ch + 1) * CHUNK)
            src_col = jnp.where(sel_blk < ML_HEADS, d * 8 + ML_HEADS + sel_blk, d * 8 + sel_blk - ML_HEADS)
            select = jnp.where(sel_row == src_col, 1.0, 0.0).astype(BF16)
            g = gc_ref[rs, :]
            g_hi = g.astype(BF16)
            g_lo = (g - g_hi.astype(F32)).astype(BF16)
            rep = (jnp.dot(g_hi, select, preferred_element_type=F32)
                   + jnp.dot(g_lo, select, preferred_element_type=F32))
            mask = (col <= row) if d == 0 else (col >= row)
            last = CHUNK - 1 if d == 0 else 0
            for h in range(ML_HEADS):
                b_rep = rep[:, h * LANES:(h + 1) * LANES]
                pm_rep = rep[:, (ML_HEADS + h) * LANES:(ML_HEADS + h + 1) * LANES]
                d_row = gr_ref[d * 8 + h:d * 8 + h + 1, rs]
                b_end = b_rep[last:last + 1, :]
                loc[ci, d, h] = dict(
                    rs=rs, b_rep=b_rep, b_end=b_end,
                    log_d=jnp.where(mask, b_rep[:, :CHUNK] + d_row, -jnp.inf),
                    log_w=b_end[:, :CHUNK] + d_row,
                    q=x_ref[rs, h * ML_DK:(h + 1) * ML_DK],
                    kt=kt_ref[h * ML_DK:(h + 1) * ML_DK, rs],
                    v1=jnp.concatenate([x_ref[rs, 2 * qw + h * ML_DV:2 * qw + (h + 1) * ML_DV], ones], axis=-1),
                    row_max=b_rep + pm_rep,
                    w_max=b_end + pm_rep[last:last + 1, :])
    for u in loc.values():
        u["qk"] = _dot(u["q"], u["kt"])

    state = {(d, h): (c_ref[d * ML_HEADS + h], m_ref[d * ML_HEADS + h]) for d, h in units}
    for ci in range(nch):
        cur = [(dh, loc[(ci,) + dh], state[dh]) for dh in units]
        for _, u, (cn, m) in cur:
            log_inter = u["b_rep"] + m
            m_t = jnp.maximum(log_inter, u["row_max"])
            u["w_inter"] = jnp.exp(log_inter - m_t)
            u["floor"] = jnp.exp(-m_t)
            u["s"] = u["qk"] * jnp.exp(u["log_d"] - m_t[:, :CHUNK]) * kscale
        for _, u, (cn, m) in cur:
            u["qc"] = _dot(u["q"], cn)
            u["sv"] = _dot(u["s"], u["v1"])
        for (d, h), u, (cn, m) in cur:
            num = u["w_inter"] * u["qc"][:, :ML_DV] + u["sv"][:, :ML_DV]
            den = u["w_inter"] * u["qc"][:, ML_DV:] + u["sv"][:, ML_DV:]
            refs[d][4][u["rs"], h * ML_DV:(h + 1) * ML_DV] = num / jnp.maximum(jnp.abs(den), u["floor"])
            m_new = jnp.maximum(u["b_end"] + m, u["w_max"])
            u["decay"] = jnp.exp(u["b_end"] + m - m_new)
            u["m_new"] = m_new
            u["kwt"] = u["kt"].astype(F32) * (jnp.exp(u["log_w"] - m_new[:, :CHUNK]) * kscale)
        for _, u, _st in cur:
            u["upd"] = _dot(u["kwt"], u["v1"])
        for dh, u, (cn, m) in cur:
            state[dh] = (jnp.concatenate([u["decay"], u["decay"]], axis=-1) * cn + u["upd"], u["m_new"])
    for d, h in units:
        st = d * ML_HEADS + h
        c_ref[st], m_ref[st] = state[(d, h)]


def _mlstm(p_ml, kt, gcol, grow, batch, seq, nctx):
    nt = p_ml.shape[0]
    nsteps, fwd, bwd = _scan_blocks(batch, seq, nctx)
    xw = 2 * ML_HEADS * ML_DK + ML_W
    nst = 2 * ML_HEADS
    return pl.pallas_call(
        _mlstm_kernel,
        grid=(batch, nsteps),
        in_specs=[
            pl.BlockSpec((TB, xw), lambda b, j: (fwd(b, j), 0)),
            pl.BlockSpec((TB, xw), lambda b, j: (bwd(b, j), 0)),
            pl.BlockSpec((ML_HEADS * ML_DK, TB), lambda b, j: (0, fwd(b, j))),
            pl.BlockSpec((ML_HEADS * ML_DK, TB), lambda b, j: (0, bwd(b, j))),
            pl.BlockSpec((TB, P_GATE), lambda b, j: (fwd(b, j), 0)),
            pl.BlockSpec((TB, P_GATE), lambda b, j: (bwd(b, j), 0)),
            pl.BlockSpec((P_GATE, TB), lambda b, j: (0, fwd(b, j))),
            pl.BlockSpec((P_GATE, TB), lambda b, j: (0, bwd(b, j))),
        ],
        out_specs=[
            pl.BlockSpec((TB, ML_W), lambda b, j: (fwd(b, j), 0)),
            pl.BlockSpec((TB, ML_W), lambda b, j: (bwd(b, j), 0)),
        ],
        out_shape=[jax.ShapeDtypeStruct((nt, ML_W), F32)] * 2,
        scratch_shapes=[
            pltpu.VMEM((nst, ML_DK, ML_DV + LANES), F32),
            pltpu.VMEM((nst, 1, LANES), F32),
        ],
        compiler_params=_cparams("parallel", "arbitrary"),
        name="mlstm",
    )(p_ml, p_ml, kt, kt, gcol, gcol, grow, grow)


INV_BLOCK = 16


def _unit_triangular_inverses(mats):
    n = mats[0].shape[0]
    row = lax.broadcasted_iota(jnp.int32, (n, n), 0)
    col = lax.broadcasted_iota(jnp.int32, (n, n), 1)
    eye = jnp.where(row == col, 1.0, 0.0)
    same_block = row // INV_BLOCK == col // INV_BLOCK
    ps = [jnp.where(same_block, a, 0.0) for a in mats]
    xs = [eye - p for p in ps]
    for _ in range((INV_BLOCK - 1).bit_length() - 1):
        ps = [_dot(p, p) for p in ps]
        xs = [x + xp for x, xp in zip(xs, [_dot(x, p) for x, p in zip(xs, ps)])]
    size = INV_BLOCK
    while size < n:
        sel = (row // (2 * size) == col // (2 * size)) & (row // size != col // size)
        ts = [_dot(x, jnp.where(sel, a, 0.0)) for x, a in zip(xs, mats)]
        xs = [x - tx for x, tx in zip(xs, [_dot(t, x) for t, x in zip(ts, xs)])]
        size *= 2
    return xs


def _gdn_kernel(xf_ref, xb_ref, gcf_ref, gcb_ref, grf_ref, grb_ref, of_ref, ob_ref, s_ref):
    @pl.when(pl.program_id(1) == 0)
    def _():
        s_ref[...] = jnp.zeros_like(s_ref)

    nch = TB // CHUNK
    row = lax.broadcasted_iota(jnp.int32, (CHUNK, CHUNK), 0)
    col = lax.broadcasted_iota(jnp.int32, (CHUNK, CHUNK), 1)
    nk = GD_HEADS * GD_DK
    refs = ((xf_ref, gcf_ref, grf_ref, of_ref), (xb_ref, gcb_ref, grb_ref, ob_ref))
    units = _scan_units(GD_HEADS)

    loc = {}
    for ci in range(nch):
        for d, h in units:
            x_ref, gc_ref, gr_ref, _ = refs[d]
            ch = ci if d == 0 else nch - 1 - ci
            rs = slice(ch * CHUNK, (ch + 1) * CHUNK)
            cb = 16 + d * 8 + h
            cg = 16 + d * 8 + 4 + h
            beta = jnp.broadcast_to(gc_ref[rs, cb:cb + 1], (CHUNK, LANES))
            g_col = jnp.broadcast_to(gc_ref[rs, cg:cg + 1], (CHUNK, LANES))
            g_row = gr_ref[cg:cg + 1, rs]
            g_end = g_col[CHUNK - 1:CHUNK, :] if d == 0 else g_col[0:1, :]
            incl = (col <= row) if d == 0 else (col >= row)
            q = x_ref[rs, h * GD_DK:(h + 1) * GD_DK]
            k = x_ref[rs, nk + h * GD_DK:nk + (h + 1) * GD_DK]
            v = x_ref[rs, 2 * nk + h * GD_DV:2 * nk + (h + 1) * GD_DV]
            kb = k * beta
            e_col = jnp.exp(g_col)
            loc[ci, d, h] = dict(
                rs=rs, k=k, kb=kb, q=q,
                strict=(col < row) if d == 0 else (col > row),
                decay=jnp.exp(jnp.where(incl, g_col[:, :CHUNK] - g_row, -jnp.inf)),
                rhs=jnp.concatenate([v * beta, kb * e_col], axis=-1),
                qg=q * e_col, kd=k * jnp.exp(g_end - g_col), e_end=jnp.exp(g_end))
    us = list(loc.values())
    for u in us:
        kk_qk = _dot(jnp.concatenate([u["kb"], u["q"]], axis=0), u["k"], NT_DIMS)
        u["kk"] = kk_qk[:CHUNK]
        u["attn"] = kk_qk[CHUNK:] * u["decay"]
    tinvs = _unit_triangular_inverses([jnp.where(u["strict"], u["kk"] * u["decay"], 0.0) for u in us])
    for u, tinv in zip(us, tinvs):
        u["uw"] = _dot(tinv, u["rhs"])

    state = {(d, h): s_ref[d * GD_HEADS + h] for d, h in units}
    for ci in range(nch):
        cur = [(dh, loc[(ci,) + dh]) for dh in units]
        for dh, u in cur:
            u["ws_qs"] = _dot(jnp.concatenate([u["uw"][:, GD_DV:], u["qg"]], axis=0), state[dh])
        for dh, u in cur:
            u["v_new"] = u["uw"][:, :GD_DV] - u["ws_qs"][:CHUNK]
            u["qs"] = u["ws_qs"][CHUNK:]
        for dh, u in cur:
            u["av"] = _dot(u["attn"], u["v_new"])
            u["upd"] = _dot(u["kd"], u["v_new"], TN_DIMS)
        for (d, h), u in cur:
            refs[d][3][u["rs"], h * GD_DV:(h + 1) * GD_DV] = u["qs"] + u["av"]
            state[(d, h)] = state[(d, h)] * u["e_end"] + u["upd"]
    for d, h in units:
        s_ref[d * GD_HEADS + h] = state[(d, h)]


def _gdn(qkv, gcol, grow, batch, seq, nctx):
    nt = qkv.shape[0]
    nsteps, fwd, bwd = _scan_blocks(batch, seq, nctx)
    cw = qkv.shape[1]
    return pl.pallas_call(
        _gdn_kernel,
        grid=(batch, nsteps),
        in_specs=[
            pl.BlockSpec((TB, cw), lambda b, j: (fwd(b, j), 0)),
            pl.BlockSpec((TB, cw), lambda b, j: (bwd(b, j), 0)),
            pl.BlockSpec((TB, P_GATE), lambda b, j: (fwd(b, j), 0)),
            pl.BlockSpec((TB, P_GATE), lambda b, j: (bwd(b, j), 0)),
            pl.BlockSpec((P_GATE, TB), lambda b, j: (0, fwd(b, j))),
            pl.BlockSpec((P_GATE, TB), lambda b, j: (0, bwd(b, j))),
        ],
        out_specs=[
            pl.BlockSpec((TB, GD_W), lambda b, j: (fwd(b, j), 0)),
            pl.BlockSpec((TB, GD_W), lambda b, j: (bwd(b, j), 0)),
        ],
        out_shape=[jax.ShapeDtypeStruct((nt, GD_W), F32)] * 2,
        scratch_shapes=[pltpu.VMEM((2 * GD_HEADS, GD_DK, GD_DV), F32)],
        compiler_params=_cparams("parallel", "arbitrary"),
        name="gdn",
    )(qkv, qkv, gcol, gcol, grow, grow)


def _head_rms(y, gain, nheads, width):
    parts = []
    for h in range(nheads):
        seg = y[:, h * width:(h + 1) * width]
        parts.append(_rms(seg))
    return jnp.concatenate(parts, axis=-1) * gain


def _merge_kernel(*refs, n_x, lat_tiles):
    x_refs = refs[:n_x]
    (mod_ref, al_ref, ac_ref, hf_ref, hb_ref, o_ref, gf_ref, gb_ref, z_ref,
     an_ref, mn_ref, gn_ref, w_ref, out_ref) = refs[n_x:]
    a = _token_tile((al_ref, ac_ref), lat_tiles).astype(F32)
    ya = _head_rms(a, an_ref[...], MLA_HEADS, V_HEAD)
    ym = _head_rms(hf_ref[...] + hb_ref[...], mn_ref[...], ML_HEADS, ML_DV) * _sigmoid(o_ref[...].astype(F32))
    z = z_ref[...].astype(F32)
    yg = _head_rms(gf_ref[...] + gb_ref[...], gn_ref[...], GD_HEADS, GD_DV) * (z * _sigmoid(z))
    y = jnp.dot(ya.astype(BF16), w_ref[0:MLA_W, :], preferred_element_type=F32)
    y = y + jnp.dot(ym.astype(BF16), w_ref[MLA_W:MLA_W + ML_W, :], preferred_element_type=F32)
    y = y + jnp.dot(yg.astype(BF16), w_ref[MLA_W + ML_W:, :], preferred_element_type=F32)
    out_ref[...] = _token_tile(x_refs, lat_tiles) + mod_ref[2:3, :] * y


def _merge(l, xs, mod, a_lat, a_ctx, hf, hb, p_ml, gf, gb, gd_z, mla_norm, ml_norm, gd_norm, w_out, seq, batch, nrows):
    d = xs[0].shape[1]
    lat_tiles = a_lat.shape[0] // TM
    assert a_ctx.shape[0] == TM
    row = functools.partial(_mod_row, seq=seq, batch=batch)
    tile = lambda w, cb=0: pl.BlockSpec((TM, w), lambda i: (i, cb))
    vec = lambda w: pl.BlockSpec((None, 1, w), lambda i: (l, 0, 0))
    return pl.pallas_call(
        functools.partial(_merge_kernel, n_x=len(xs), lat_tiles=lat_tiles),
        grid=(nrows // TM,),
        in_specs=_token_specs(xs, d) + [
            pl.BlockSpec((None, None, 6, d), lambda i: (l, row(i), 0, 0)),
            *_token_specs((a_lat, a_ctx), MLA_W),
            tile(ML_W), tile(ML_W), tile(ML_W, 2),
            tile(GD_W), tile(GD_W), tile(GD_W),
            vec(MLA_W), vec(ML_W), vec(GD_W),
            pl.BlockSpec((None, MIX_W, d), lambda i: (l, 0, 0)),
        ],
        out_specs=tile(d),
        out_shape=jax.ShapeDtypeStruct((nrows, d), F32),
        compiler_params=_cparams("parallel"),
        name="merge",
    )(*xs, mod, a_lat, a_ctx, hf, hb, p_ml, gf, gb, gd_z, mla_norm[:, None, :], ml_norm[:, None, :],
      jnp.tile(gd_norm, (1, GD_HEADS))[:, None, :], w_out)


def _mlp_kernel(x_ref, mod_ref, nw_ref, w1_ref, w2_ref, *rest):
    out_ref, h_ref = rest[-2:]
    j = pl.program_id(1)

    @pl.when(j == 0)
    def _():
        h_ref[...] = _norm_mod(x_ref[...], nw_ref[...], mod_ref[3:4, :], mod_ref[4:5, :]).astype(BF16)
        out_ref[...] = jnp.zeros_like(out_ref)

    a = jnp.maximum(jnp.dot(h_ref[...], w1_ref[...], preferred_element_type=F32), 0.0)
    out_ref[...] += jnp.dot((a * a).astype(BF16), w2_ref[...], preferred_element_type=F32)

    @pl.when(j == pl.num_programs(1) - 1)
    def _():
        y = x_ref[...] + mod_ref[5:6, :] * out_ref[...]
        out_ref[...] = _rms(y) * rest[0][...] if len(rest) == 3 else y


def _mlp(l, xg, mod, norm_w, w1, w2, seq, batch, nrows, final_w=None):
    d = xg.shape[1]
    f = w1.shape[-1]
    tf = min(MLP_TF, f)
    row = functools.partial(_mod_row, seq=seq, batch=batch)
    extra = [] if final_w is None else [final_w.reshape(1, d)]
    return pl.pallas_call(
        _mlp_kernel,
        grid=(nrows // TM, f // tf),
        in_specs=[
            pl.BlockSpec((TM, d), lambda i, j: (i, 0)),
            pl.BlockSpec((None, None, 6, d), lambda i, j: (l, row(i), 0, 0)),
            pl.BlockSpec((None, 1, d), lambda i, j: (l, 0, 0)),
            pl.BlockSpec((None, d, tf), lambda i, j: (l, 0, j)),
            pl.BlockSpec((None, tf, d), lambda i, j: (l, j, 0)),
        ] + [pl.BlockSpec((1, d), lambda i, j: (0, 0)) for _ in extra],
        out_specs=pl.BlockSpec((TM, d), lambda i, j: (i, 0)),
        out_shape=jax.ShapeDtypeStruct((nrows, d), F32),
        scratch_shapes=[pltpu.VMEM((TM, d), BF16)],
        compiler_params=_cparams("parallel", "arbitrary"),
        name="mlp",
    )(xg, mod, norm_w[:, None, :], w1, w2, *extra)


def _regroup_w_in_kernel(wt_ref, o_ref):
    o_ml = Q_LORA + KV_LORA + QK_ROPE
    o_mlg = o_ml + P_ML
    o_gd = o_mlg + 4 * ML_HEADS
    o_gdg = o_gd + P_GD
    tr = wt_ref.shape[1]
    eye = jnp.where(lax.broadcasted_iota(jnp.int32, (tr, tr), 0) == lax.broadcasted_iota(jnp.int32, (tr, tr), 1),
                    1.0, 0.0).astype(BF16)
    zeros = lambda n: jnp.zeros((tr, n), BF16)
    piece = lambda a, b: lax.dot_general(eye, wt_ref[a:b, :].astype(BF16), NT_DIMS,
                                         preferred_element_type=F32).astype(BF16)
    o_ref[...] = jnp.concatenate([
        piece(0, o_ml), zeros(P_MLA - o_ml),
        piece(o_ml, o_mlg),
        piece(o_gd, o_gdg),
        piece(o_mlg, o_gd), piece(o_gdg, o_gdg + 4 * GD_HEADS), zeros(P_GATE - 4 * ML_HEADS - 4 * GD_HEADS),
    ], axis=-1)


def _regroup_w_in(w_in):
    nl, d, n_in = w_in.shape
    tr = min(512, d)
    return pl.pallas_call(
        _regroup_w_in_kernel,
        grid=(nl, d // tr),
        in_specs=[pl.BlockSpec((None, n_in, tr), lambda l, i: (l, 0, i))],
        out_specs=pl.BlockSpec((None, tr, P_ALL), lambda l, i: (l, i, 0)),
        out_shape=jax.ShapeDtypeStruct((nl, d, P_ALL), BF16),
        compiler_params=_cparams("parallel", "parallel"),
        name="regroup_w_in",
    )(jnp.swapaxes(w_in, 1, 2))


def _regroup_heads(w, first):
    nl, kdim, _ = w.shape
    w4 = w.reshape(nl, kdim, MLA_HEADS, -1)
    return jnp.concatenate([w4[..., :first].reshape(nl, kdim, -1), w4[..., first:].reshape(nl, kdim, -1)],
                           axis=-1).astype(BF16)


def _rope_tables(seq):
    half = QK_ROPE // 2
    t = jnp.arange(seq, dtype=jnp.int32)
    inv = ROPE_BASE ** (-jnp.arange(0, half, 2, dtype=F32) / half)
    ang_r = (t // GRID_W).astype(F32)[:, None] * inv
    ang_c = (t % GRID_W).astype(F32)[:, None] * inv
    cos = jnp.concatenate([jnp.cos(ang_r)] * 2 + [jnp.cos(ang_c)] * 2, axis=-1)
    sin = jnp.concatenate([-jnp.sin(ang_r), jnp.sin(ang_r), -jnp.sin(ang_c), jnp.sin(ang_c)], axis=-1)
    cos = jnp.concatenate([cos, jnp.ones((TM, QK_ROPE), F32)], axis=0)
    sin = jnp.concatenate([sin, jnp.zeros((TM, QK_ROPE), F32)], axis=0)
    return jnp.tile(cos, (1, MLA_HEADS)), jnp.tile(sin, (1, MLA_HEADS))


def _gate_params(ml_gate_bias, gd_a_log, gd_dt_bias):
    nl = ml_gate_bias.shape[0]
    zh = jnp.zeros((nl, GD_HEADS), F32)
    bias = jnp.concatenate([ml_gate_bias, zh, gd_dt_bias[:, 0], zh, gd_dt_bias[:, 1]], axis=-1)
    alog = jnp.concatenate([jnp.zeros((nl, 4 * ML_HEADS), F32), zh, gd_a_log[:, 0], zh, gd_a_log[:, 1]], axis=-1)
    par = jnp.stack([bias, alog], axis=1)
    return jnp.pad(par, ((0, 0), (0, 6), (0, P_GATE - par.shape[-1])))


def kernel(x, c, ctx, c_ctx, w_ada, b_ada, norm1, norm2, w_in, mla_q_norm, mla_w_uq, mla_kv_norm, mla_w_ukv, mla_out_norm, ml_gate_bias, ml_out_norm, gd_conv, gd_a_log, gd_dt_bias, gd_out_norm, w_out, w_mlp1, w_mlp2, final_norm):
    batch, seq, d = x.shape
    nctx = ctx.shape[1]
    depth = w_ada.shape[0]
    n_lat = batch * seq
    n_all = n_lat + batch * nctx
    assert seq % TM == 0 and (batch * nctx) % TM == 0 and nctx % TB == 0 and TM % nctx == 0

    xs = (x.reshape(n_lat, d), ctx.reshape(batch * nctx, d))
    mod_all = _ada(jnp.concatenate([c, c_ctx[None]], axis=0), w_ada, b_ada).reshape(depth, batch + 1, 6, d)
    w_in_g = _regroup_w_in(w_in)
    wuq_g = _regroup_heads(mla_w_uq, QK_NOPE)
    wukv_g = _regroup_heads(mla_w_ukv, QK_NOPE)
    w_out_b = w_out.astype(BF16)
    w1_b = w_mlp1.astype(BF16)
    w2_b = w_mlp2.astype(BF16)
    cos_t, sin_t = _rope_tables(seq)
    gate_par = _gate_params(ml_gate_bias, gd_a_log, gd_dt_bias)

    for l in range(depth):
        last = l == depth - 1
        p_mla, p_ml, ml_kt, gd_qkv, gd_z, gcol, grow = _inproj(l, xs, mod_all, norm1, w_in_g, gate_par, gd_conv,
                                                               seq, batch, nctx)
        q, k, v1 = _mla_qkv_call(l, p_mla, mla_q_norm, mla_kv_norm, wuq_g, wukv_g, cos_t, sin_t, seq, n_lat)
        a_lat, a_ctx = _attention(q, k, v1, batch, seq, nctx)
        hf, hb = _mlstm(p_ml, ml_kt, gcol, grow, batch, seq, nctx)
        gf, gb = _gdn(gd_qkv, gcol, grow, batch, seq, nctx)
        nrows = n_lat if last else n_all
        xg = _merge(l, xs, mod_all, a_lat, a_ctx, hf, hb, p_ml, gf, gb, gd_z, mla_out_norm, ml_out_norm,
                    gd_out_norm, w_out_b, seq, batch, nrows)
        xs = (_mlp(l, xg, mod_all, norm2, w1_b, w2_b, seq, batch, nrows, final_norm if last else None),)
    return xs[0].reshape(batch, seq, d)
```

```python
import functools

import jax
import jax.numpy as jnp
from jax import lax
from jax.experimental import pallas as pl
from jax.experimental.pallas import tpu as pltpu

F32 = jnp.float32
BF16 = jnp.bfloat16

EPS = 1e-6
LOG2_E = 1.4426950408889634
M_INIT = -1e30
GRID_W = 64
ROPE_BASE = 10000.0

MLA_HEADS = 8
Q_LORA = 512
KV_LORA = 256
QK_NOPE = 128
QK_ROPE = 64
V_HEAD = 128
QK_DIM = QK_NOPE + QK_ROPE
ML_HEADS = 4
ML_DK = 64
ML_DV = 128
GD_HEADS = 4
GD_DK = 128
GD_DV = 128
CHUNK = 64
CONV_W = 5

MLA_W = MLA_HEADS * V_HEAD
ML_W = ML_HEADS * ML_DV
GD_W = GD_HEADS * GD_DV
MIX_W = MLA_W + ML_W + GD_W

P_MLA = 896
P_ML = 1536
P_GD = 2048
P_GATE = 128
P_ALL = P_MLA + P_ML + P_GD + P_GATE

TM = 512
MLP_TF = 2048
ADA_COLS = 1024
TB = 256
HALO = 16
GATE_ROWS = 128
LANES = 128

VMEM_LIMIT_V7X = 58 * 1024 * 1024

NT_DIMS = (((1,), (1,)), ((), ()))
TN_DIMS = (((0,), (0,)), ((), ()))
NN_DIMS = (((1,), (0,)), ((), ()))


def _cparams(*sem):
    return pltpu.CompilerParams(dimension_semantics=sem, vmem_limit_bytes=VMEM_LIMIT_V7X)


def _sigmoid(x):
    return 1.0 / (1.0 + jnp.exp(-x))


def _rms(x):
    return x * lax.rsqrt(jnp.mean(x * x, axis=-1, keepdims=True) + EPS)


def _norm_mod(x, norm_w, shift, scale):
    return _rms(x) * norm_w * (1.0 + scale) + shift


def _mod_row(i, seq, batch):
    return jnp.minimum((i * TM) // seq, batch)


def _ada_kernel(cb_ref, w_ref, b_ref, o_ref, s_ref):
    @pl.when((pl.program_id(0) == 0) & (pl.program_id(1) == 0))
    def _():
        cb = cb_ref[...]
        s_ref[...] = cb * _sigmoid(cb)

    rows, d = cb_ref.shape[0], cb_ref.shape[1]
    tn = w_ref.shape[-1]
    tc = next(t for t in (ADA_COLS, ADA_COLS // 2, LANES) if tn % t == 0)
    sub = 8
    for c0 in range(0, tn, tc):
        def body(kb, accs):
            k0 = pl.multiple_of(kb * sub, sub)
            w = w_ref[pl.ds(k0, sub), c0:c0 + tc]
            return tuple(acc + w * jnp.tile(s_ref[r, pl.ds(k0, sub), :], (1, tc // LANES))
                         for r, acc in enumerate(accs))

        accs = lax.fori_loop(0, d // sub, body, tuple(jnp.zeros((sub, tc), F32) for _ in range(rows)), unroll=4)
        for r in range(rows):
            o_ref[r:r + 1, c0:c0 + tc] = jnp.sum(accs[r], axis=0, keepdims=True) + b_ref[:, c0:c0 + tc]


def _ada(cvec, w_ada, b_ada):
    nl, d, n6 = w_ada.shape
    r = cvec.shape[0]
    tn = next(t for t in (2048, 1536, 1024, 512, LANES) if n6 % t == 0)
    cb = jnp.broadcast_to(cvec[:, :, None], (r, d, LANES))
    return pl.pallas_call(
        _ada_kernel,
        grid=(nl, n6 // tn),
        in_specs=[
            pl.BlockSpec((r, d, LANES), lambda l, j: (0, 0, 0)),
            pl.BlockSpec((None, d, tn), lambda l, j: (l, 0, j)),
            pl.BlockSpec((None, 1, tn), lambda l, j: (l, 0, j)),
        ],
        out_specs=pl.BlockSpec((None, r, tn), lambda l, j: (l, 0, j)),
        out_shape=jax.ShapeDtypeStruct((nl, r, n6), F32),
        scratch_shapes=[pltpu.VMEM((r, d, LANES), F32)],
        compiler_params=_cparams("arbitrary", "arbitrary"),
        name="ada",
    )(cb, w_ada, b_ada.reshape(nl, 1, n6))


def _token_specs(xs, width):
    if len(xs) == 1:
        return [pl.BlockSpec((TM, width), lambda i: (i, 0))]
    nl = xs[0].shape[0] // TM
    return [pl.BlockSpec((TM, width), lambda i: (jnp.minimum(i, nl - 1), 0)),
            pl.BlockSpec((TM, width), lambda i: (jnp.maximum(i - nl, 0), 0))]


def _token_tile(x_refs, lat_tiles):
    if len(x_refs) == 1:
        return x_refs[0][...]
    return jnp.where(pl.program_id(0) < lat_tiles, x_refs[0][...], x_refs[1][...])


def _gate_columns(g, bias, a_log):
    tb = g.shape[0]
    z = g + bias
    neg_a = -jnp.exp(a_log)
    lane = lax.broadcasted_iota(jnp.int32, z.shape, 1)
    kind = (lane // 4) % 4
    is_ml = lane < 16
    is_cum = (kind % 2 == 1) & (lane < 32)
    is_bwd = kind >= 2
    soft = jnp.log(1.0 + jnp.exp(-jnp.abs(z)))
    log_sig = jnp.minimum(z, 0.0) - soft
    softplus = jnp.maximum(z, 0.0) + soft
    pre = jnp.where(is_ml, jnp.where(is_cum, log_sig, z), jnp.where(is_cum, neg_a * softplus, _sigmoid(z)))
    r = lax.broadcasted_iota(jnp.int32, (tb, tb), 0)
    c = lax.broadcasted_iota(jnp.int32, (tb, tb), 1)
    same = (r // CHUNK) == (c // CHUNK)
    lower = jnp.where(same & (c <= r), 1.0, 0.0).astype(BF16)
    upper = jnp.where(same & (c >= r), 1.0, 0.0).astype(BF16)
    hi = pre.astype(BF16)
    rest = pre - hi.astype(F32)
    mid = rest.astype(BF16)
    parts = (hi, mid, (rest - mid.astype(F32)).astype(BF16))
    cum_f = sum(jnp.dot(lower, p, preferred_element_type=F32) for p in parts)
    cum_b = sum(jnp.dot(upper, p, preferred_element_type=F32) for p in parts)
    out = jnp.where(is_cum, jnp.where(is_bwd, cum_b, cum_f), pre)
    is_mli = is_ml & jnp.logical_not(is_cum)
    d = out - pltpu.roll(out, P_GATE - ML_HEADS, 1)
    t = lax.broadcasted_iota(jnp.int32, z.shape, 0) % CHUNK
    run = d
    shift = 1
    while shift < CHUNK:
        prev_f = jnp.where(t >= shift, pltpu.roll(run, shift, 0), -jnp.inf)
        prev_b = jnp.where(t < CHUNK - shift, pltpu.roll(run, tb - shift, 0), -jnp.inf)
        run = jnp.maximum(run, jnp.where(is_bwd, prev_b, prev_f))
        shift *= 2
    return jnp.where(is_mli, run, out), jnp.where(is_mli, d, out).T


def _short_conv_heads(buf_ref, cw_ref, tap_ok):
    half = CONV_W // 2
    y = None
    for j in range(CONV_W):
        term = buf_ref[HALO - half + j:HALO - half + j + TM, :] * cw_ref[j:j + 1, :]
        if tap_ok[j] is not None:
            term = term * tap_ok[j]
        y = term if y is None else y + term
    y = y * _sigmoid(y)
    nk = GD_HEADS * GD_DK
    parts = []
    for h in range(2 * GD_HEADS):
        seg = y[:, h * GD_DK:(h + 1) * GD_DK]
        seg = seg * lax.rsqrt(jnp.sum(seg * seg, axis=-1, keepdims=True) + EPS)
        parts.append(seg * (GD_DK ** -0.5) if h < GD_HEADS else seg)
    return jnp.concatenate(parts + [y[:, 2 * nk:]], axis=-1)


def _inproj_kernel(*refs, n_x, lat_tiles, seq_tiles, nctx):
    x_refs = refs[:n_x]
    (xp_ref, xn_ref, mod_ref, nw_ref, w_ref, gpar_ref, cw_ref,
     omla_ref, oml_ref, kt_ref, gdq_ref, z_ref, gcol_ref, grow_ref, buf_ref) = refs[n_x:]
    i = pl.program_id(0)
    in_lat = i < lat_tiles
    first = jnp.logical_or(jnp.logical_not(in_lat), i % seq_tiles == 0)
    last = jnp.logical_or(jnp.logical_not(in_lat), i % seq_tiles == seq_tiles - 1)

    x_ext = jnp.concatenate([xp_ref[...], _token_tile(x_refs, lat_tiles), xn_ref[...]], axis=0)
    h_ext = _norm_mod(x_ext, nw_ref[...], mod_ref[0:1, :], mod_ref[1:2, :]).astype(BF16)
    h = h_ext[HALO:HALO + TM]
    proj = lambda lhs, c0, n: jnp.dot(lhs, w_ref[:, c0:c0 + n], preferred_element_type=F32)

    c_gd = P_MLA + P_ML
    qkv_w = 2 * GD_HEADS * GD_DK + GD_W
    p_qkv = proj(h_ext, c_gd, qkv_w)
    buf_ref[0:HALO, :] = jnp.where(first, 0.0, p_qkv[0:HALO])
    buf_ref[HALO:HALO + TM, :] = p_qkv[HALO:HALO + TM]
    buf_ref[HALO + TM:, :] = jnp.where(last, 0.0, p_qkv[HALO + TM:])
    gates = proj(h, c_gd + P_GD, P_GATE)
    omla_ref[...] = proj(h, 0, P_MLA).astype(omla_ref.dtype)
    p_ml = proj(h, P_MLA, P_ML)
    oml_ref[...] = p_ml.astype(oml_ref.dtype)
    kw = ML_HEADS * ML_DK
    kt_ref[...] = p_ml[:, kw:2 * kw].T.astype(kt_ref.dtype)
    z_ref[...] = proj(h, c_gd + qkv_w, GD_W).astype(z_ref.dtype)

    pos = lax.broadcasted_iota(jnp.int32, (TM, LANES), 0) % nctx
    reps = qkv_w // LANES
    tap_ok = []
    for j in range(CONV_W):
        o = j - CONV_W // 2
        crosses = (pos + o < 0) | (pos + o >= nctx)
        keep = jnp.where(jnp.logical_or(in_lat, jnp.logical_not(crosses)), 1.0, 0.0)
        tap_ok.append(None if o == 0 or TM <= nctx else jnp.tile(keep, (1, reps)))
    gdq_ref[...] = _short_conv_heads(buf_ref, cw_ref, tap_ok).astype(gdq_ref.dtype)
    for r0 in range(0, TM, GATE_ROWS):
        col, row_t = _gate_columns(gates[r0:r0 + GATE_ROWS], gpar_ref[0:1, :], gpar_ref[1:2, :])
        gcol_ref[r0:r0 + GATE_ROWS, :] = col
        grow_ref[:, r0:r0 + GATE_ROWS] = row_t


def _inproj(l, xs, mod, norm_w, w, gate_par, conv_w, seq, batch, nctx):
    nt = sum(x.shape[0] for x in xs)
    d = xs[0].shape[1]
    row = functools.partial(_mod_row, seq=seq, batch=batch)
    lat_tiles = batch * seq // TM
    hpt = TM // HALO
    n_halo = xs[0].shape[0] // HALO
    qkv_w = 2 * GD_HEADS * GD_DK + GD_W
    kw = ML_HEADS * ML_DK
    tile = lambda n: pl.BlockSpec((TM, n), lambda i: (i, 0))
    outs = [
        (tile(P_MLA), (nt, P_MLA), BF16),
        (tile(P_ML), (nt, P_ML), BF16),
        (pl.BlockSpec((kw, TM), lambda i: (0, i)), (kw, nt), BF16),
        (tile(qkv_w), (nt, qkv_w), BF16),
        (tile(GD_W), (nt, GD_W), BF16),
        (tile(P_GATE), (nt, P_GATE), F32),
        (pl.BlockSpec((P_GATE, TM), lambda i: (0, i)), (P_GATE, nt), F32),
    ]
    kern = functools.partial(_inproj_kernel, n_x=len(xs), lat_tiles=lat_tiles, seq_tiles=seq // TM, nctx=nctx)
    return pl.pallas_call(
        kern,
        grid=(nt // TM,),
        in_specs=_token_specs(xs, d) + [
            pl.BlockSpec((HALO, d), lambda i: (jnp.clip(i * hpt - 1, 0, n_halo - 1), 0)),
            pl.BlockSpec((HALO, d), lambda i: (jnp.clip((i + 1) * hpt, 0, n_halo - 1), 0)),
            pl.BlockSpec((None, None, 6, d), lambda i: (l, row(i), 0, 0)),
            pl.BlockSpec((None, 1, d), lambda i: (l, 0, 0)),
            pl.BlockSpec((None, d, P_ALL), lambda i: (l, 0, 0), pipeline_mode=pl.Buffered(1)),
            pl.BlockSpec((None, 8, P_GATE), lambda i: (l, 0, 0)),
            pl.BlockSpec((None, CONV_W, qkv_w), lambda i: (l, 0, 0)),
        ],
        out_specs=[o[0] for o in outs],
        out_shape=[jax.ShapeDtypeStruct(o[1], o[2]) for o in outs],
        scratch_shapes=[pltpu.VMEM((TM + 2 * HALO, qkv_w), F32)],
        compiler_params=_cparams("parallel"),
        name="inproj",
    )(*xs, xs[0], xs[0], mod, norm_w[:, None, :], w, gate_par, conv_w)


def _swap16(x):
    n = x.shape[-1]
    lane = lax.broadcasted_iota(jnp.int32, x.shape, 1)
    return jnp.where(lane % 32 < 16, pltpu.roll(x, n - 16, 1), pltpu.roll(x, 16, 1))


def _mla_qkv_kernel(p_ref, qn_ref, kvn_ref, wuq_ref, wukv_ref, cos_ref, sin_ref, q_ref, k_ref, v_ref):
    p = p_ref[...].astype(F32)
    cq = p[:, :Q_LORA]
    ckv = p[:, Q_LORA:Q_LORA + KV_LORA]
    kpe = p[:, Q_LORA + KV_LORA:]
    cos = cos_ref[...]
    sin = sin_ref[...]
    qn = (_rms(cq) * qn_ref[...]).astype(BF16)
    q = jnp.dot(qn, wuq_ref[...], preferred_element_type=F32) * (QK_DIM ** -0.5 * LOG2_E)
    nope_w = MLA_HEADS * QK_NOPE
    qpe = q[:, nope_w:]
    qpe = qpe * cos + _swap16(qpe) * sin
    kvn = (_rms(ckv) * kvn_ref[...]).astype(BF16)
    kv = jnp.dot(kvn, wukv_ref[...], preferred_element_type=F32)
    kpe = kpe * cos[:, :LANES] + _swap16(kpe) * sin[:, :LANES]
    kpe = kpe[:, :QK_ROPE]
    ones = jnp.ones((q.shape[0], LANES), BF16)
    for h in range(MLA_HEADS):
        qh = jnp.concatenate([q[:, h * QK_NOPE:(h + 1) * QK_NOPE], qpe[:, h * QK_ROPE:(h + 1) * QK_ROPE]], axis=-1)
        kh = jnp.concatenate([kv[:, h * QK_NOPE:(h + 1) * QK_NOPE], kpe], axis=-1)
        q_ref[h] = qh.astype(BF16)
        k_ref[h] = kh.astype(BF16)
        v_ref[h, :, :V_HEAD] = kv[:, nope_w + h * V_HEAD:nope_w + (h + 1) * V_HEAD].astype(BF16)
        v_ref[h, :, V_HEAD:] = ones


def _mla_qkv_call(l, p_mla, q_norm, kv_norm, wuq, wukv, cos_t, sin_t, seq, n_lat_rows):
    nt = p_mla.shape[0]
    lat_tiles = seq // TM
    nl_tiles = n_lat_rows // TM
    pe_w = MLA_HEADS * QK_ROPE
    v1_w = V_HEAD + LANES

    def rope_block(i):
        return (jnp.where(i < nl_tiles, i % lat_tiles, lat_tiles), 0)

    return pl.pallas_call(
        _mla_qkv_kernel,
        grid=(nt // TM,),
        in_specs=[
            pl.BlockSpec((TM, P_MLA), lambda i: (i, 0)),
            pl.BlockSpec((None, 1, Q_LORA), lambda i: (l, 0, 0)),
            pl.BlockSpec((None, 1, KV_LORA), lambda i: (l, 0, 0)),
            pl.BlockSpec((None,) + wuq.shape[1:], lambda i: (l, 0, 0)),
            pl.BlockSpec((None,) + wukv.shape[1:], lambda i: (l, 0, 0)),
            pl.BlockSpec((TM, pe_w), rope_block),
            pl.BlockSpec((TM, pe_w), rope_block),
        ],
        out_specs=[
            pl.BlockSpec((MLA_HEADS, TM, QK_DIM), lambda i: (0, i, 0)),
            pl.BlockSpec((MLA_HEADS, TM, QK_DIM), lambda i: (0, i, 0)),
            pl.BlockSpec((MLA_HEADS, TM, v1_w), lambda i: (0, i, 0)),
        ],
        out_shape=[
            jax.ShapeDtypeStruct((MLA_HEADS, nt, QK_DIM), BF16),
            jax.ShapeDtypeStruct((MLA_HEADS, nt, QK_DIM), BF16),
            jax.ShapeDtypeStruct((MLA_HEADS, nt, v1_w), BF16),
        ],
        compiler_params=_cparams("parallel"),
        name="mla_qkv",
    )(p_mla, q_norm[:, None, :], kv_norm[:, None, :], wuq, wukv, cos_t, sin_t)


KV_CHUNK = 512
ATT_TQ = 1024


def _attn_kernel(q_ref, *refs):
    o_ref = refs[-1]
    q = q_ref[...]
    chunks = []
    for k_ref, v_ref in zip(refs[0:-1:2], refs[1:-1:2]):
        n = k_ref.shape[0]
        step = min(KV_CHUNK, n)
        chunks += [(k_ref, v_ref, c0, step) for c0 in range(0, n, step)]
    scores = lambda c: lax.dot_general(q, c[0][c[2]:c[2] + c[3], :], NT_DIMS, preferred_element_type=F32)
    m = jnp.full((q.shape[0], 1), -jnp.inf, F32)
    acc = jnp.zeros((q.shape[0], V_HEAD + LANES), F32)
    s_next = scores(chunks[0])
    for idx, c in enumerate(chunks):
        s = s_next
        if idx + 1 < len(chunks):
            s_next = scores(chunks[idx + 1])
        m_new = jnp.maximum(m, jnp.max(s, axis=-1, keepdims=True))
        p = jnp.exp2(s - m_new).astype(BF16)
        acc = jnp.exp2(m - m_new) * acc + jnp.dot(p, c[1][c[2]:c[2] + c[3], :], preferred_element_type=F32)
        m = m_new
    o_ref[...] = (acc[:, :V_HEAD] / acc[:, V_HEAD:]).astype(o_ref.dtype)


def _attention(q, k, v1, batch, seq, nctx):
    tq = ATT_TQ
    nq = seq // tq
    ctx0 = batch * seq // nctx
    v1_w = v1.shape[-1]
    a_lat = pl.pallas_call(
        _attn_kernel,
        grid=(batch, MLA_HEADS, nq),
        in_specs=[
            pl.BlockSpec((None, tq, QK_DIM), lambda b, h, i: (h, b * nq + i, 0)),
            pl.BlockSpec((None, nctx, QK_DIM), lambda b, h, i: (h, ctx0 + b, 0)),
            pl.BlockSpec((None, nctx, v1_w), lambda b, h, i: (h, ctx0 + b, 0)),
            pl.BlockSpec((None, seq, QK_DIM), lambda b, h, i: (h, b, 0)),
            pl.BlockSpec((None, seq, v1_w), lambda b, h, i: (h, b, 0)),
        ],
        out_specs=pl.BlockSpec((tq, V_HEAD), lambda b, h, i: (b * nq + i, h)),
        out_shape=jax.ShapeDtypeStruct((batch * seq, MLA_W), BF16),
        compiler_params=_cparams("parallel", "parallel", "arbitrary"),
        name="attn_lat",
    )(q, k, v1, k, v1)
    a_ctx = pl.pallas_call(
        _attn_kernel,
        grid=(batch, MLA_HEADS),
        in_specs=[
            pl.BlockSpec((None, nctx, QK_DIM), lambda b, h: (h, ctx0 + b, 0)),
            pl.BlockSpec((None, nctx, QK_DIM), lambda b, h: (h, ctx0 + b, 0)),
            pl.BlockSpec((None, nctx, v1_w), lambda b, h: (h, ctx0 + b, 0)),
        ],
        out_specs=pl.BlockSpec((nctx, V_HEAD), lambda b, h: (b, h)),
        out_shape=jax.ShapeDtypeStruct((batch * nctx, MLA_W), BF16),
        compiler_params=_cparams("parallel", "parallel"),
        name="attn_ctx",
    )(q, k, v1)
    return a_lat, a_ctx


def _scan_blocks(batch, seq, nctx):
    ncb = nctx // TB
    nlb = seq // TB
    ctx0 = batch * nlb

    def fwd(b, j):
        return jnp.where(j < ncb, ctx0 + b * ncb + j, b * nlb + (j - ncb))

    def bwd(b, j):
        return jnp.where(j < ncb, ctx0 + b * ncb + (ncb - 1 - j), b * nlb + (nlb - 1 - (j - ncb)))

    return ncb + nlb, fwd, bwd


def _dot(a, b, dims=NN_DIMS):
    return lax.dot_general(a.astype(BF16), b.astype(BF16), dims, preferred_element_type=F32)


def _scan_units(nheads):
    return [(d, h) for d in range(2) for h in range(nheads)]


def _mlstm_kernel(xf_ref, xb_ref, ktf_ref, ktb_ref, gcf_ref, gcb_ref, grf_ref, grb_ref, hf_ref, hb_ref, c_ref, m_ref):
    @pl.when(pl.program_id(1) == 0)
    def _():
        c_ref[...] = jnp.zeros_like(c_ref)
        m_ref[...] = jnp.full(m_ref.shape, M_INIT, F32)

    nch = TB // CHUNK
    row = lax.broadcasted_iota(jnp.int32, (CHUNK, CHUNK), 0)
    col = lax.broadcasted_iota(jnp.int32, (CHUNK, CHUNK), 1)
    kscale = ML_DK ** -0.5
    qw = ML_HEADS * ML_DK
    refs = ((xf_ref, ktf_ref, gcf_ref, grf_ref, hf_ref), (xb_ref, ktb_ref, gcb_ref, grb_ref, hb_ref))
    units = _scan_units(ML_HEADS)

    ones = jnp.ones((CHUNK, LANES), BF16)
    sel_row = lax.broadcasted_iota(jnp.int32, (P_GATE, 2 * ML_HEADS * LANES), 0)
    sel_blk = lax.broadcasted_iota(jnp.int32, (P_GATE, 2 * ML_HEADS * LANES), 1) // LANES
    loc = {}
    for ci in range(nch):
        for d in range(2):
            x_ref, kt_ref, gc_ref, gr_ref, _ = refs[d]
            ch = ci if d == 0 else nch - 1 - ci
            rs = slice(ch * CHUNK, (ch + 1) * CHUNK)
            src_col = jnp.where(sel_blk < ML_HEADS, d * 8 + ML_HEADS + sel_blk, d * 8 + sel_blk - ML_HEADS)
            select = jnp.where(sel_row == src_col, 1.0, 0.0).astype(BF16)
            g = gc_ref[rs, :]
            g_hi = g.astype(BF16)
            g_lo = (g - g_hi.astype(F32)).astype(BF16)
            rep = (jnp.dot(g_hi, select, preferred_element_type=F32)
                   + jnp.dot(g_lo, select, preferred_element_type=F32))
            mask = (col <= row) if d == 0 else (col >= row)
            last = CHUNK - 1 if d == 0 else 0
            for h in range(ML_HEADS):
                b_rep = rep[:, h * LANES:(h + 1) * LANES]
                pm_rep = rep[:, (ML_HEADS + h) * LANES:(ML_HEADS + h + 1) * LANES]
                d_row = gr_ref[d * 8 + h:d * 8 + h + 1, rs]
                b_end = b_rep[last:last + 1, :]
                loc[ci, d, h] = dict(
                    rs=rs, b_rep=b_rep, b_end=b_end,
                    log_d=jnp.where(mask, b_rep[:, :CHUNK] + d_row, -jnp.inf),
                    log_w=b_end[:, :CHUNK] + d_row,
                    q=x_ref[rs, h * ML_DK:(h + 1) * ML_DK],
                    kt=kt_ref[h * ML_DK:(h + 1) * ML_DK, rs],
                    v1=jnp.concatenate([x_ref[rs, 2 * qw + h * ML_DV:2 * qw + (h + 1) * ML_DV], ones], axis=-1),
                    row_max=b_rep + pm_rep,
                    w_max=b_end + pm_rep[last:last + 1, :])
    for u in loc.values():
        u["qk"] = _dot(u["q"], u["kt"])

    state = {(d, h): (c_ref[d * ML_HEADS + h], m_ref[d * ML_HEADS + h]) for d, h in units}
    for ci in range(nch):
        cur = [(dh, loc[(ci,) + dh], state[dh]) for dh in units]
        for _, u, (cn, m) in cur:
            log_inter = u["b_rep"] + m
            m_t = jnp.maximum(log_inter, u["row_max"])
            u["w_inter"] = jnp.exp(log_inter - m_t)
            u["floor"] = jnp.exp(-m_t)
            u["s"] = u["qk"] * jnp.exp(u["log_d"] - m_t[:, :CHUNK]) * kscale
        for _, u, (cn, m) in cur:
            u["qc"] = _dot(u["q"], cn)
            u["sv"] = _dot(u["s"], u["v1"])
        for (d, h), u, (cn, m) in cur:
            num = u["w_inter"] * u["qc"][:, :ML_DV] + u["sv"][:, :ML_DV]
            den = u["w_inter"] * u["qc"][:, ML_DV:] + u["sv"][:, ML_DV:]
            refs[d][4][u["rs"], h * ML_DV:(h + 1) * ML_DV] = num / jnp.maximum(jnp.abs(den), u["floor"])
            m_new = jnp.maximum(u["b_end"] + m, u["w_max"])
            u["decay"] = jnp.exp(u["b_end"] + m - m_new)
            u["m_new"] = m_new
            u["kwt"] = u["kt"].astype(F32) * (jnp.exp(u["log_w"] - m_new[:, :CHUNK]) * kscale)
        for _, u, _st in cur:
            u["upd"] = _dot(u["kwt"], u["v1"])
        for dh, u, (cn, m) in cur:
            state[dh] = (jnp.concatenate([u["decay"], u["decay"]], axis=-1) * cn + u["upd"], u["m_new"])
    for d, h in units:
        st = d * ML_HEADS + h
        c_ref[st], m_ref[st] = state[(d, h)]


def _mlstm(p_ml, kt, gcol, grow, batch, seq, nctx):
    nt = p_ml.shape[0]
    nsteps, fwd, bwd = _scan_blocks(batch, seq, nctx)
    xw = 2 * ML_HEADS * ML_DK + ML_W
    nst = 2 * ML_HEADS
    return pl.pallas_call(
        _mlstm_kernel,
        grid=(batch, nsteps),
        in_specs=[
            pl.BlockSpec((TB, xw), lambda b, j: (fwd(b, j), 0)),
            pl.BlockSpec((TB, xw), lambda b, j: (bwd(b, j), 0)),
            pl.BlockSpec((ML_HEADS * ML_DK, TB), lambda b, j: (0, fwd(b, j))),
            pl.BlockSpec((ML_HEADS * ML_DK, TB), lambda b, j: (0, bwd(b, j))),
            pl.BlockSpec((TB, P_GATE), lambda b, j: (fwd(b, j), 0)),
            pl.BlockSpec((TB, P_GATE), lambda b, j: (bwd(b, j), 0)),
            pl.BlockSpec((P_GATE, TB), lambda b, j: (0, fwd(b, j))),
            pl.BlockSpec((P_GATE, TB), lambda b, j: (0, bwd(b, j))),
        ],
        out_specs=[
            pl.BlockSpec((TB, ML_W), lambda b, j: (fwd(b, j), 0)),
            pl.BlockSpec((TB, ML_W), lambda b, j: (bwd(b, j), 0)),
        ],
        out_shape=[jax.ShapeDtypeStruct((nt, ML_W), F32)] * 2,
        scratch_shapes=[
            pltpu.VMEM((nst, ML_DK, ML_DV + LANES), F32),
            pltpu.VMEM((nst, 1, LANES), F32),
        ],
        compiler_params=_cparams("parallel", "arbitrary"),
        name="mlstm",
    )(p_ml, p_ml, kt, kt, gcol, gcol, grow, grow)


INV_BLOCK = 16


def _unit_triangular_inverses(mats):
    n = mats[0].shape[0]
    row = lax.broadcasted_iota(jnp.int32, (n, n), 0)
    col = lax.broadcasted_iota(jnp.int32, (n, n), 1)
    eye = jnp.where(row == col, 1.0, 0.0)
    same_block = row // INV_BLOCK == col // INV_BLOCK
    ps = [jnp.where(same_block, a, 0.0) for a in mats]
    xs = [eye - p for p in ps]
    for _ in range((INV_BLOCK - 1).bit_length() - 1):
        ps = [_dot(p, p) for p in ps]
        xs = [x + xp for x, xp in zip(xs, [_dot(x, p) for x, p in zip(xs, ps)])]
    size = INV_BLOCK
    while size < n:
        sel = (row // (2 * size) == col // (2 * size)) & (row // size != col // size)
        ts = [_dot(x, jnp.where(sel, a, 0.0)) for x, a in zip(xs, mats)]
        xs = [x - tx for x, tx in zip(xs, [_dot(t, x) for t, x in zip(ts, xs)])]
        size *= 2
    return xs


def _gdn_kernel(xf_ref, xb_ref, gcf_ref, gcb_ref, grf_ref, grb_ref, of_ref, ob_ref, s_ref):
    @pl.when(pl.program_id(1) == 0)
    def _():
        s_ref[...] = jnp.zeros_like(s_ref)

    nch = TB // CHUNK
    row = lax.broadcasted_iota(jnp.int32, (CHUNK, CHUNK), 0)
    col = lax.broadcasted_iota(jnp.int32, (CHUNK, CHUNK), 1)
    nk = GD_HEADS * GD_DK
    refs = ((xf_ref, gcf_ref, grf_ref, of_ref), (xb_ref, gcb_ref, grb_ref, ob_ref))
    units = _scan_units(GD_HEADS)

    loc = {}
    for ci in range(nch):
        for d, h in units:
            x_ref, gc_ref, gr_ref, _ = refs[d]
            ch = ci if d == 0 else nch - 1 - ci
            rs = slice(ch * CHUNK, (ch + 1) * CHUNK)
            cb = 16 + d * 8 + h
            cg = 16 + d * 8 + 4 + h
            beta = jnp.broadcast_to(gc_ref[rs, cb:cb + 1], (CHUNK, LANES))
            g_col = jnp.broadcast_to(gc_ref[rs, cg:cg + 1], (CHUNK, LANES))
            g_row = gr_ref[cg:cg + 1, rs]
            g_end = g_col[CHUNK - 1:CHUNK, :] if d == 0 else g_col[0:1, :]
            incl = (col <= row) if d == 0 else (col >= row)
            q = x_ref[rs, h * GD_DK:(h + 1) * GD_DK]
            k = x_ref[rs, nk + h * GD_DK:nk + (h + 1) * GD_DK]
            v = x_ref[rs, 2 * nk + h * GD_DV:2 * nk + (h + 1) * GD_DV]
            kb = k * beta
            e_col = jnp.exp(g_col)
            loc[ci, d, h] = dict(
                rs=rs, k=k, kb=kb, q=q,
                strict=(col < row) if d == 0 else (col > row),
                decay=jnp.exp(jnp.where(incl, g_col[:, :CHUNK] - g_row, -jnp.inf)),
                rhs=jnp.concatenate([v * beta, kb * e_col], axis=-1),
                qg=q * e_col, kd=k * jnp.exp(g_end - g_col), e_end=jnp.exp(g_end))
    us = list(loc.values())
    for u in us:
        kk_qk = _dot(jnp.concatenate([u["kb"], u["q"]], axis=0), u["k"], NT_DIMS)
        u["kk"] = kk_qk[:CHUNK]
        u["attn"] = kk_qk[CHUNK:] * u["decay"]
    tinvs = _unit_triangular_inverses([jnp.where(u["strict"], u["kk"] * u["decay"], 0.0) for u in us])
    for u, tinv in zip(us, tinvs):
        u["uw"] = _dot(tinv, u["rhs"])

    state = {(d, h): s_ref[d * GD_HEADS + h] for d, h in units}
    for ci in range(nch):
        cur = [(dh, loc[(ci,) + dh]) for dh in units]
        for dh, u in cur:
            u["ws_qs"] = _dot(jnp.concatenate([u["uw"][:, GD_DV:], u["qg"]], axis=0), state[dh])
        for dh, u in cur:
            u["v_new"] = u["uw"][:, :GD_DV] - u["ws_qs"][:CHUNK]
            u["qs"] = u["ws_qs"][CHUNK:]
        for dh, u in cur:
            u["av"] = _dot(u["attn"], u["v_new"])
            u["upd"] = _dot(u["kd"], u["v_new"], TN_DIMS)
        for (d, h), u in cur:
            refs[d][3][u["rs"], h * GD_DV:(h + 1) * GD_DV] = u["qs"] + u["av"]
            state[(d, h)] = state[(d, h)] * u["e_end"] + u["upd"]
    for d, h in units:
        s_ref[d * GD_HEADS + h] = state[(d, h)]


def _gdn(qkv, gcol, grow, batch, seq, nctx):
    nt = qkv.shape[0]
    nsteps, fwd, bwd = _scan_blocks(batch, seq, nctx)
    cw = qkv.shape[1]
    return pl.pallas_call(
        _gdn_kernel,
        grid=(batch, nsteps),
        in_specs=[
            pl.BlockSpec((TB, cw), lambda b, j: (fwd(b, j), 0)),
            pl.BlockSpec((TB, cw), lambda b, j: (bwd(b, j), 0)),
            pl.BlockSpec((TB, P_GATE), lambda b, j: (fwd(b, j), 0)),
            pl.BlockSpec((TB, P_GATE), lambda b, j: (bwd(b, j), 0)),
            pl.BlockSpec((P_GATE, TB), lambda b, j: (0, fwd(b, j))),
            pl.BlockSpec((P_GATE, TB), lambda b, j: (0, bwd(b, j))),
        ],
        out_specs=[
            pl.BlockSpec((TB, GD_W), lambda b, j: (fwd(b, j), 0)),
            pl.BlockSpec((TB, GD_W), lambda b, j: (bwd(b, j), 0)),
        ],
        out_shape=[jax.ShapeDtypeStruct((nt, GD_W), F32)] * 2,
        scratch_shapes=[pltpu.VMEM((2 * GD_HEADS, GD_DK, GD_DV), F32)],
        compiler_params=_cparams("parallel", "arbitrary"),
        name="gdn",
    )(qkv, qkv, gcol, gcol, grow, grow)


def _head_rms(y, gain, nheads, width):
    parts = []
    for h in range(nheads):
        seg = y[:, h * width:(h + 1) * width]
        parts.append(_rms(seg))
    return jnp.concatenate(parts, axis=-1) * gain


def _merge_kernel(*refs, n_x, lat_tiles):
    x_refs = refs[:n_x]
    (mod_ref, al_ref, ac_ref, hf_ref, hb_ref, o_ref, gf_ref, gb_ref, z_ref,
     an_ref, mn_ref, gn_ref, w_ref, out_ref) = refs[n_x:]
    a = _token_tile((al_ref, ac_ref), lat_tiles).astype(F32)
    ya = _head_rms(a, an_ref[...], MLA_HEADS, V_HEAD)
    ym = _head_rms(hf_ref[...] + hb_ref[...], mn_ref[...], ML_HEADS, ML_DV) * _sigmoid(o_ref[...].astype(F32))
    z = z_ref[...].astype(F32)
    yg = _head_rms(gf_ref[...] + gb_ref[...], gn_ref[...], GD_HEADS, GD_DV) * (z * _sigmoid(z))
    y = jnp.dot(ya.astype(BF16), w_ref[0:MLA_W, :], preferred_element_type=F32)
    y = y + jnp.dot(ym.astype(BF16), w_ref[MLA_W:MLA_W + ML_W, :], preferred_element_type=F32)
    y = y + jnp.dot(yg.astype(BF16), w_ref[MLA_W + ML_W:, :], preferred_element_type=F32)
    out_ref[...] = _token_tile(x_refs, lat_tiles) + mod_ref[2:3, :] * y


def _merge(l, xs, mod, a_lat, a_ctx, hf, hb, p_ml, gf, gb, gd_z, mla_norm, ml_norm, gd_norm, w_out, seq, batch, nrows):
    d = xs[0].shape[1]
    lat_tiles = a_lat.shape[0] // TM
    assert a_ctx.shape[0] == TM
    row = functools.partial(_mod_row, seq=seq, batch=batch)
    tile = lambda w, cb=0: pl.BlockSpec((TM, w), lambda i: (i, cb))
    vec = lambda w: pl.BlockSpec((None, 1, w), lambda i: (l, 0, 0))
    return pl.pallas_call(
        functools.partial(_merge_kernel, n_x=len(xs), lat_tiles=lat_tiles),
        grid=(nrows // TM,),
        in_specs=_token_specs(xs, d) + [
            pl.BlockSpec((None, None, 6, d), lambda i: (l, row(i), 0, 0)),
            *_token_specs((a_lat, a_ctx), MLA_W),
            tile(ML_W), tile(ML_W), tile(ML_W, 2),
            tile(GD_W), tile(GD_W), tile(GD_W),
            vec(MLA_W), vec(ML_W), vec(GD_W),
            pl.BlockSpec((None, MIX_W, d), lambda i: (l, 0, 0)),
        ],
        out_specs=tile(d),
        out_shape=jax.ShapeDtypeStruct((nrows, d), F32),
        compiler_params=_cparams("parallel"),
        name="merge",
    )(*xs, mod, a_lat, a_ctx, hf, hb, p_ml, gf, gb, gd_z, mla_norm[:, None, :], ml_norm[:, None, :],
      jnp.tile(gd_norm, (1, GD_HEADS))[:, None, :], w_out)


def _mlp_kernel(x_ref, mod_ref, nw_ref, w1_ref, w2_ref, *rest):
    out_ref, h_ref = rest[-2:]
    j = pl.program_id(1)

    @pl.when(j == 0)
    def _():
        h_ref[...] = _norm_mod(x_ref[...], nw_ref[...], mod_ref[3:4, :], mod_ref[4:5, :]).astype(BF16)
        out_ref[...] = jnp.zeros_like(out_ref)

    a = jnp.maximum(jnp.dot(h_ref[...], w1_ref[...], preferred_element_type=F32), 0.0)
    out_ref[...] += jnp.dot((a * a).astype(BF16), w2_ref[...], preferred_element_type=F32)

    @pl.when(j == pl.num_programs(1) - 1)
    def _():
        y = x_ref[...] + mod_ref[5:6, :] * out_ref[...]
        out_ref[...] = _rms(y) * rest[0][...] if len(rest) == 3 else y


def _mlp(l, xg, mod, norm_w, w1, w2, seq, batch, nrows, final_w=None):
    d = xg.shape[1]
    f = w1.shape[-1]
    tf = min(MLP_TF, f)
    row = functools.partial(_mod_row, seq=seq, batch=batch)
    extra = [] if final_w is None else [final_w.reshape(1, d)]
    return pl.pallas_call(
        _mlp_kernel,
        grid=(nrows // TM, f // tf),
        in_specs=[
            pl.BlockSpec((TM, d), lambda i, j: (i, 0)),
            pl.BlockSpec((None, None, 6, d), lambda i, j: (l, row(i), 0, 0)),
            pl.BlockSpec((None, 1, d), lambda i, j: (l, 0, 0)),
            pl.BlockSpec((None, d, tf), lambda i, j: (l, 0, j)),
            pl.BlockSpec((None, tf, d), lambda i, j: (l, j, 0)),
        ] + [pl.BlockSpec((1, d), lambda i, j: (0, 0)) for _ in extra],
        out_specs=pl.BlockSpec((TM, d), lambda i, j: (i, 0)),
        out_shape=jax.ShapeDtypeStruct((nrows, d), F32),
        scratch_shapes=[pltpu.VMEM((TM, d), BF16)],
        compiler_params=_cparams("parallel", "arbitrary"),
        name="mlp",
    )(xg, mod, norm_w[:, None, :], w1, w2, *extra)


def _regroup_w_in_kernel(wt_ref, o_ref):
    o_ml = Q_LORA + KV_LORA + QK_ROPE
    o_mlg = o_ml + P_ML
    o_gd = o_mlg + 4 * ML_HEADS
    o_gdg = o_gd + P_GD
    tr = wt_ref.shape[1]
    eye = jnp.where(lax.broadcasted_iota(jnp.int32, (tr, tr), 0) == lax.broadcasted_iota(jnp.int32, (tr, tr), 1),
                    1.0, 0.0).astype(BF16)
    zeros = lambda n: jnp.zeros((tr, n), BF16)
    piece = lambda a, b: lax.dot_general(eye, wt_ref[a:b, :].astype(BF16), NT_DIMS,
                                         preferred_element_type=F32).astype(BF16)
    o_ref[...] = jnp.concatenate([
        piece(0, o_ml), zeros(P_MLA - o_ml),
        piece(o_ml, o_mlg),
        piece(o_gd, o_gdg),
        piece(o_mlg, o_gd), piece(o_gdg, o_gdg + 4 * GD_HEADS), zeros(P_GATE - 4 * ML_HEADS - 4 * GD_HEADS),
    ], axis=-1)


def _regroup_w_in(w_in):
    nl, d, n_in = w_in.shape
    tr = min(512, d)
    return pl.pallas_call(
        _regroup_w_in_kernel,
        grid=(nl, d // tr),
        in_specs=[pl.BlockSpec((None, n_in, tr), lambda l, i: (l, 0, i))],
        out_specs=pl.BlockSpec((None, tr, P_ALL), lambda l, i: (l, i, 0)),
        out_shape=jax.ShapeDtypeStruct((nl, d, P_ALL), BF16),
        compiler_params=_cparams("parallel", "parallel"),
        name="regroup_w_in",
    )(jnp.swapaxes(w_in, 1, 2))


def _regroup_heads(w, first):
    nl, kdim, _ = w.shape
    w4 = w.reshape(nl, kdim, MLA_HEADS, -1)
    return jnp.concatenate([w4[..., :first].reshape(nl, kdim, -1), w4[..., first:].reshape(nl, kdim, -1)],
                           axis=-1).astype(BF16)


def _rope_tables(seq):
    half = QK_ROPE // 2
    t = jnp.arange(seq, dtype=jnp.int32)
    inv = ROPE_BASE ** (-jnp.arange(0, half, 2, dtype=F32) / half)
    ang_r = (t // GRID_W).astype(F32)[:, None] * inv
    ang_c = (t % GRID_W).astype(F32)[:, None] * inv
    cos = jnp.concatenate([jnp.cos(ang_r)] * 2 + [jnp.cos(ang_c)] * 2, axis=-1)
    sin = jnp.concatenate([-jnp.sin(ang_r), jnp.sin(ang_r), -jnp.sin(ang_c), jnp.sin(ang_c)], axis=-1)
    cos = jnp.concatenate([cos, jnp.ones((TM, QK_ROPE), F32)], axis=0)
    sin = jnp.concatenate([sin, jnp.zeros((TM, QK_ROPE), F32)], axis=0)
    return jnp.tile(cos, (1, MLA_HEADS)), jnp.tile(sin, (1, MLA_HEADS))


def _gate_params(ml_gate_bias, gd_a_log, gd_dt_bias):
    nl = ml_gate_bias.shape[0]
    zh = jnp.zeros((nl, GD_HEADS), F32)
    bias = jnp.concatenate([ml_gate_bias, zh, gd_dt_bias[:, 0], zh, gd_dt_bias[:, 1]], axis=-1)
    alog = jnp.concatenate([jnp.zeros((nl, 4 * ML_HEADS), F32), zh, gd_a_log[:, 0], zh, gd_a_log[:, 1]], axis=-1)
    par = jnp.stack([bias, alog], axis=1)
    return jnp.pad(par, ((0, 0), (0, 6), (0, P_GATE - par.shape[-1])))


def kernel(x, c, ctx, c_ctx, w_ada, b_ada, norm1, norm2, w_in, mla_q_norm, mla_w_uq, mla_kv_norm, mla_w_ukv, mla_out_norm, ml_gate_bias, ml_out_norm, gd_conv, gd_a_log, gd_dt_bias, gd_out_norm, w_out, w_mlp1, w_mlp2, final_norm):
    batch, seq, d = x.shape
    nctx = ctx.shape[1]
    depth = w_ada.shape[0]
    n_lat = batch * seq
    n_all = n_lat + batch * nctx
    assert seq % TM == 0 and (batch * nctx) % TM == 0 and nctx % TB == 0 and TM % nctx == 0

    xs = (x.reshape(n_lat, d), ctx.reshape(batch * nctx, d))
    mod_all = _ada(jnp.concatenate([c, c_ctx[None]], axis=0), w_ada, b_ada).reshape(depth, batch + 1, 6, d)
    w_in_g = _regroup_w_in(w_in)
    wuq_g = _regroup_heads(mla_w_uq, QK_NOPE)
    wukv_g = _regroup_heads(mla_w_ukv, QK_NOPE)
    w_out_b = w_out.astype(BF16)
    w1_b = w_mlp1.astype(BF16)
    w2_b = w_mlp2.astype(BF16)
    cos_t, sin_t = _rope_tables(seq)
    gate_par = _gate_params(ml_gate_bias, gd_a_log, gd_dt_bias)

    for l in range(depth):
        last = l == depth - 1
        p_mla, p_ml, ml_kt, gd_qkv, gd_z, gcol, grow = _inproj(l, xs, mod_all, norm1, w_in_g, gate_par, gd_conv,
                                                               seq, batch, nctx)
        q, k, v1 = _mla_qkv_call(l, p_mla, mla_q_norm, mla_kv_norm, wuq_g, wukv_g, cos_t, sin_t, seq, n_lat)
        a_lat, a_ctx = _attention(q, k, v1, batch, seq, nctx)
        hf, hb = _mlstm(p_ml, ml_kt, gcol, grow, batch, seq, nctx)
        gf, gb = _gdn(gd_qkv, gcol, grow, batch, seq, nctx)
        nrows = n_lat if last else n_all
        xg = _merge(l, xs, mod_all, a_lat, a_ctx, hf, hb, p_ml, gf, gb, gd_z, mla_out_norm, ml_out_norm,
                    gd_out_norm, w_out_b, seq, batch, nrows)
        xs = (_mlp(l, xg, mod_all, norm2, w1_b, w2_b, seq, batch, nrows, final_norm if last else None),)
    return xs[0].reshape(batch, seq, d)
```
